```python
import math
import jax, jax.numpy as jnp
from jax import lax
import numpy as np

D_MODEL = 4096
BATCH = 4
SEQ = 2048
DEPTH = 2
DEC_BATCH = 128
DEC_SEQ = 8
PAST_LEN = 16384
PAGE_SIZE = 128

GMLP_WIDTH = D_MODEL // 2
CHUNK = 128
GMLP_HEAD_DIM = 128
GMLP_HEADS = GMLP_WIDTH // GMLP_HEAD_DIM
SSM_WIDTH = D_MODEL // 2
SSM_GROUP_CH = 16
SSM_GROUPS = SSM_WIDTH // SSM_GROUP_CH
SSM_STATE = 64
DT_MIN = 1e-3
DT_MAX = 1e-1
IN_COLS = 2 * GMLP_WIDTH + SSM_WIDTH + 2 * D_MODEL
SPLITS = (GMLP_WIDTH, 2 * GMLP_WIDTH, 2 * GMLP_WIDTH + SSM_WIDTH, 2 * GMLP_WIDTH + SSM_WIDTH + D_MODEL)
D_FF_DENSE = 11008
N_EXPERTS = 8
TOP_K = 2
D_FF_EXPERT = 14336
N_DENSE = (DEPTH + 1) // 2
N_MOE = DEPTH // 2
RMS_EPS = 1e-5
LN_EPS = 1e-5

kernel_name = "hybrid_gmlp_s5_moe_decode_step"


def rmsnorm(x, g):
    xf = x.astype(jnp.float32)
    xf = xf * lax.rsqrt(jnp.mean(xf * xf, axis=-1, keepdims=True) + RMS_EPS)
    return (xf * g.astype(jnp.float32)).astype(x.dtype)


def layernorm(x, g, b):
    xf = x.astype(jnp.float32)
    mu = jnp.mean(xf, axis=-1, keepdims=True)
    var = jnp.mean(jnp.square(xf - mu), axis=-1, keepdims=True)
    y = (xf - mu) * lax.rsqrt(var + LN_EPS) * g.astype(jnp.float32) + b.astype(jnp.float32)
    return y.astype(x.dtype)


def gmlp_spatial(u, v, w_s, b_s):
    n, L, _ = v.shape
    n_chunks = -(-L // CHUNK)
    pad = n_chunks * CHUNK - L
    vp = jnp.pad(v, ((0, 0), (0, pad), (0, 0))).reshape(n, n_chunks, CHUNK, GMLP_HEADS, GMLP_HEAD_DIM)
    causal = jnp.tril(jnp.ones((CHUNK, CHUNK), dtype=bool))
    w = jnp.where(causal[None], w_s, jnp.zeros((), w_s.dtype))
    mixed = jnp.einsum('hts,bnshd->bnthd', w, vp) + jnp.transpose(b_s)[None, None, :, :, None]
    mixed = mixed.reshape(n, n_chunks * CHUNK, GMLP_WIDTH)[:, :L]
    return u * mixed


def s5_scan(u, h0, a_re, a_im, log_dt, b_re, b_im, c_re, c_im, d_skip):
    n, L, _ = u.shape
    f32 = jnp.float32
    ug = u.astype(f32).reshape(n, L, SSM_GROUPS, SSM_GROUP_CH).astype(jnp.complex64)
    lam = lax.complex(a_re.astype(f32), a_im.astype(f32))
    dt = jnp.exp(log_dt.astype(f32))[:, None]
    a_bar = jnp.exp(lam * dt)
    bmat = lax.complex(b_re.astype(f32), b_im.astype(f32))
    b_bar = ((a_bar - 1.0) / lam)[..., None] * bmat
    bu = jnp.einsum('gpc,nlgc->nlgp', b_bar, ug)
    bu = bu.at[:, 0].add(a_bar[None] * h0)
    a_seq = jnp.broadcast_to(a_bar, bu.shape)

    def combine(left, right):
        al, bl = left
        ar, br = right
        return al * ar, ar * bl + br

    _, h = lax.associative_scan(combine, (a_seq, bu), axis=1)
    cmat = lax.complex(c_re.astype(f32), c_im.astype(f32))
    y = jnp.einsum('gcp,nlgp->nlgc', cmat, h).real.reshape(n, L, SSM_WIDTH)
    y = y + d_skip.astype(f32) * u.astype(f32)
    return y.astype(u.dtype), h[:, -1]


def mixer(xn, h0, w_in, ln_g, ln_b, w_s, b_s, w_proj_a, a_re, a_im, log_dt,
          b_re, b_im, c_re, c_im, d_skip, w_glu, b_glu, w_proj_b, w_out):
    proj = xn @ w_in
    u_a, v_a, x_b, g_a, g_b = jnp.split(proj, SPLITS, axis=-1)
    v = layernorm(jax.nn.gelu(v_a), ln_g, ln_b)
    branch_a = gmlp_spatial(jax.nn.gelu(u_a), v, w_s, b_s) @ w_proj_a
    y, h_last = s5_scan(x_b, h0, a_re, a_im, log_dt, b_re, b_im, c_re, c_im, d_skip)
    y = jax.nn.gelu(y)
    branch_b = (y * jax.nn.sigmoid(y @ w_glu + b_glu)) @ w_proj_b
    merged = jax.nn.sigmoid(g_a) * branch_a + jax.nn.sigmoid(g_b) * branch_b
    return merged @ w_out, v, h_last


def swiglu(x, w_gate, w_up, w_down):
    return (jax.nn.silu(x @ w_gate) * (x @ w_up)) @ w_down


def moe(x, w_router, w_gate, w_up, w_down):
    n, L, d = x.shape
    xt = x.reshape(n * L, d)
    probs = jax.nn.softmax((xt @ w_router).astype(jnp.float32), axis=-1)
    top_p, top_i = lax.top_k(probs, TOP_K)
    top_p = top_p / jnp.sum(top_p, axis=-1, keepdims=True)
    combine = jnp.sum(jax.nn.one_hot(top_i, N_EXPERTS, dtype=jnp.float32) * top_p[..., None], axis=1)
    combine = combine.astype(x.dtype)
    out = jnp.zeros_like(xt)
    for e in range(N_EXPERTS):
        out = out + combine[:, e:e + 1] * swiglu(xt, w_gate[e], w_up[e], w_down[e])
    return out.reshape(n, L, d)


def forward(x, ssm_re, ssm_im, keep_v, p):
    new_re, new_im, v_rows = [], [], []
    for l in range(DEPTH):
        xn = rmsnorm(x, p['norm_mix'][l])
        h0 = lax.complex(ssm_re[l].astype(jnp.float32), ssm_im[l].astype(jnp.float32))
        m, v, h_last = mixer(xn, h0, p['w_in'][l], p['gmlp_ln_g'][l], p['gmlp_ln_b'][l],
                             p['gmlp_w_s'][l], p['gmlp_b_s'][l], p['w_proj_a'][l],
                             p['ssm_a_re'][l], p['ssm_a_im'][l], p['ssm_log_dt'][l],
                             p['ssm_b_re'][l], p['ssm_b_im'][l], p['ssm_c_re'][l], p['ssm_c_im'][l],
                             p['ssm_d'][l], p['w_glu'][l], p['b_glu'][l], p['w_proj_b'][l], p['w_out'][l])
        x = x + m
        xn = rmsnorm(x, p['norm_ffn'][l])
        j = l // 2
        if l % 2 == 0:
            x = x + swiglu(xn, p['ffn_w_gate'][j], p['ffn_w_up'][j], p['ffn_w_down'][j])
        else:
            x = x + moe(xn, p['moe_router'][j], p['moe_w_gate'][j], p['moe_w_up'][j], p['moe_w_down'][j])
        new_re.append(h_last.real)
        new_im.append(h_last.imag)
        if keep_v:
            v_rows.append(v)
    y = rmsnorm(x, p['norm_final'])
    v_out = jnp.stack(v_rows) if keep_v else None
    return y, jnp.stack(new_re), jnp.stack(new_im), v_out


def setup_inputs(seed: int = 0) -> dict:
    key = jax.random.key(seed)
    ks = jax.random.split(key, 40)
    f32 = jnp.float32
    nrm = lambda k, shape, s: jax.random.normal(k, shape, f32) * s
    inp = {}
    inp['x_prompt'] = nrm(ks[0], (BATCH, SEQ, D_MODEL), 1.0)
    inp['x_sample'] = nrm(ks[1], (DEC_BATCH, DEC_SEQ, D_MODEL), 1.0)
    inp['state_ssm_re'] = nrm(ks[2], (DEPTH, DEC_BATCH, SSM_GROUPS, SSM_STATE), 0.1)
    inp['state_ssm_im'] = nrm(ks[3], (DEPTH, DEC_BATCH, SSM_GROUPS, SSM_STATE), 0.1)
    inp['norm_mix'] = 1.0 + nrm(ks[4], (DEPTH, D_MODEL), 0.01)
    inp['norm_ffn'] = 1.0 + nrm(ks[5], (DEPTH, D_MODEL), 0.01)
    inp['norm_final'] = 1.0 + nrm(ks[6], (D_MODEL,), 0.01)
    inp['w_in'] = nrm(ks[7], (DEPTH, D_MODEL, IN_COLS), D_MODEL ** -0.5)
    inp['gmlp_ln_g'] = 1.0 + nrm(ks[8], (DEPTH, GMLP_WIDTH), 0.01)
    inp['gmlp_ln_b'] = nrm(ks[9], (DEPTH, GMLP_WIDTH), 0.01)
    inp['gmlp_w_s'] = nrm(ks[10], (DEPTH, GMLP_HEADS, CHUNK, CHUNK), 0.5 * CHUNK ** -0.5)
    inp['gmlp_b_s'] = 1.0 + nrm(ks[11], (DEPTH, GMLP_HEADS, CHUNK), 0.01)
    inp['w_proj_a'] = nrm(ks[12], (DEPTH, GMLP_WIDTH, D_MODEL), GMLP_WIDTH ** -0.5)
    inp['ssm_a_re'] = -0.5 + nrm(ks[13], (DEPTH, SSM_GROUPS, SSM_STATE), 0.01)
    inp['ssm_a_im'] = math.pi * jnp.arange(SSM_STATE, dtype=f32) + nrm(ks[14], (DEPTH, SSM_GROUPS, SSM_STATE), 0.01)
    inp['ssm_log_dt'] = jax.random.uniform(ks[15], (DEPTH, SSM_GROUPS), f32, math.log(DT_MIN), math.log(DT_MAX))
    inp['ssm_b_re'] = nrm(ks[16], (DEPTH, SSM_GROUPS, SSM_STATE, SSM_GROUP_CH), (2 * SSM_GROUP_CH) ** -0.5)
    inp['ssm_b_im'] = nrm(ks[17], (DEPTH, SSM_GROUPS, SSM_STATE, SSM_GROUP_CH), (2 * SSM_GROUP_CH) ** -0.5)
    inp['ssm_c_re'] = nrm(ks[18], (DEPTH, SSM_GROUPS, SSM_GROUP_CH, SSM_STATE), SSM_STATE ** -0.5)
    inp['ssm_c_im'] = nrm(ks[19], (DEPTH, SSM_GROUPS, SSM_GROUP_CH, SSM_STATE), SSM_STATE ** -0.5)
    inp['ssm_d'] = nrm(ks[20], (DEPTH, SSM_WIDTH), 1.0)
    inp['w_glu'] = nrm(ks[21], (DEPTH, SSM_WIDTH, SSM_WIDTH), SSM_WIDTH ** -0.5)
    inp['b_glu'] = nrm(ks[22], (DEPTH, SSM_WIDTH), 0.01)
    inp['w_proj_b'] = nrm(ks[23], (DEPTH, SSM_WIDTH, D_MODEL), SSM_WIDTH ** -0.5)
    inp['w_out'] = nrm(ks[24], (DEPTH, D_MODEL, D_MODEL), D_MODEL ** -0.5)
    inp['ffn_w_gate'] = nrm(ks[25], (N_DENSE, D_MODEL, D_FF_DENSE), D_MODEL ** -0.5)
    inp['ffn_w_up'] = nrm(ks[26], (N_DENSE, D_MODEL, D_FF_DENSE), D_MODEL ** -0.5)
    inp['ffn_w_down'] = nrm(ks[27], (N_DENSE, D_FF_DENSE, D_MODEL), D_FF_DENSE ** -0.5)
    inp['moe_router'] = nrm(ks[28], (N_MOE, D_MODEL, N_EXPERTS), D_MODEL ** -0.5)
    inp['moe_w_gate'] = nrm(ks[29], (N_MOE, N_EXPERTS, D_MODEL, D_FF_EXPERT), D_MODEL ** -0.5)
    inp['moe_w_up'] = nrm(ks[30], (N_MOE, N_EXPERTS, D_MODEL, D_FF_EXPERT), D_MODEL ** -0.5)
    inp['moe_w_down'] = nrm(ks[31], (N_MOE, N_EXPERTS, D_FF_EXPERT, D_MODEL), D_FF_EXPERT ** -0.5)
    return inp


def reference(x_prompt, x_sample, state_ssm_re, state_ssm_im, norm_mix, norm_ffn, norm_final,
              w_in, gmlp_ln_g, gmlp_ln_b, gmlp_w_s, gmlp_b_s, w_proj_a,
              ssm_a_re, ssm_a_im, ssm_log_dt, ssm_b_re, ssm_b_im, ssm_c_re, ssm_c_im, ssm_d,
              w_glu, b_glu, w_proj_b, w_out,
              ffn_w_gate, ffn_w_up, ffn_w_down,
              moe_router, moe_w_gate, moe_w_up, moe_w_down):
    p = {'norm_mix': norm_mix, 'norm_ffn': norm_ffn, 'norm_final': norm_final, 'w_in': w_in,
         'gmlp_ln_g': gmlp_ln_g, 'gmlp_ln_b': gmlp_ln_b, 'gmlp_w_s': gmlp_w_s, 'gmlp_b_s': gmlp_b_s,
         'w_proj_a': w_proj_a, 'ssm_a_re': ssm_a_re, 'ssm_a_im': ssm_a_im, 'ssm_log_dt': ssm_log_dt,
         'ssm_b_re': ssm_b_re, 'ssm_b_im': ssm_b_im, 'ssm_c_re': ssm_c_re, 'ssm_c_im': ssm_c_im,
         'ssm_d': ssm_d, 'w_glu': w_glu, 'b_glu': b_glu, 'w_proj_b': w_proj_b, 'w_out': w_out,
         'ffn_w_gate': ffn_w_gate, 'ffn_w_up': ffn_w_up, 'ffn_w_down': ffn_w_down,
         'moe_router': moe_router, 'moe_w_gate': moe_w_gate, 'moe_w_up': moe_w_up, 'moe_w_down': moe_w_down}
    zero_state = jnp.zeros((DEPTH, BATCH, SSM_GROUPS, SSM_STATE), jnp.float32)
    y_prompt, ssm_re_prompt, ssm_im_prompt, _ = forward(x_prompt, zero_state, zero_state, False, p)
    y_sample, ssm_re_sample, ssm_im_sample, gmlp_v_sample = forward(x_sample, state_ssm_re, state_ssm_im, True, p)
    return (y_prompt, y_sample, ssm_re_prompt, ssm_im_prompt, ssm_re_sample, ssm_im_sample, gmlp_v_sample)
```

```python
import functools
import math

import jax
import jax.numpy as jnp
from jax import lax
from jax.experimental import pallas as pl
from jax.experimental.pallas import tpu as pltpu

F32 = jnp.float32
BF16 = jnp.bfloat16

RMS_EPS = 1e-5
LN_EPS = 1e-5
TOP_K = 2

LANES = 128
SSM_BLOCK = 8
GROUPS_PER_BUNDLE = 8
V7X_VMEM_LIMIT = 56 * 1024 * 1024


def _params(n_axes, vmem=V7X_VMEM_LIMIT):
    return pltpu.CompilerParams(dimension_semantics=("arbitrary",) * n_axes, vmem_limit_bytes=vmem)


def _tile(n, pref, mult=LANES):
    if n <= pref:
        return n
    t = (pref // mult) * mult
    while t >= mult:
        if n % t == 0:
            return t
        t -= mult
    raise ValueError(f"no tile for {n} <= {pref}")


def _gelu(x):
    return 0.5 * x * (1.0 + jnp.tanh(0.7978845608028654 * (x + 0.044715 * (x * x * x))))


def _sigmoid(x):
    return 1.0 / (1.0 + jnp.exp(-x))


def _dot(a, b):
    return jnp.dot(a, b, preferred_element_type=F32)


def _dot3(a, b):
    ah = a.astype(BF16)
    al = (a - ah.astype(F32)).astype(BF16)
    bh = b.astype(BF16)
    bl = (b - bh.astype(F32)).astype(BF16)
    return _dot(ah, bh) + _dot(ah, bl) + _dot(al, bh)


def _rmsnorm_body(x_ref, g_ref, o_ref):
    x = x_ref[...]
    inv = lax.rsqrt(jnp.mean(x * x, axis=-1, keepdims=True) + RMS_EPS)
    o_ref[...] = (x * inv * g_ref[...]).astype(o_ref.dtype)


def _rmsnorm(x, g, out_dtype):
    t, d = x.shape
    bm = _tile(t, 256, 8)
    return pl.pallas_call(
        _rmsnorm_body,
        out_shape=jax.ShapeDtypeStruct((t, d), out_dtype),
        grid=(t // bm,),
        in_specs=[pl.BlockSpec((bm, d), lambda i: (i, 0)), pl.BlockSpec((1, d), lambda i: (0, 0))],
        out_specs=pl.BlockSpec((bm, d), lambda i: (i, 0)),
        compiler_params=_params(1),
        name="rmsnorm",
    )(x, g.reshape(1, d))


def _mm_body(*refs, n_extra, epilogue):
    a_ref, w_ref = refs[0], refs[1]
    extras = refs[2:2 + n_extra]
    o_ref = refs[2 + n_extra]
    acc = _dot(a_ref[...].astype(BF16), w_ref[...].astype(BF16))
    if epilogue is not None:
        acc = epilogue(acc, *[r[...] for r in extras])
    o_ref[...] = acc.astype(o_ref.dtype)


def _matmul(a, w, lead, out_dtype, *, bm=1024, bn=512, epilogue=None, extras=(), a_buffers=2):
    m, k = a.shape
    n = w.shape[-1]
    assert w.shape[-2] == k
    bm = _tile(m, bm, 8)
    bn = _tile(n, bn)
    lead = tuple(lead)
    a_kwargs = {} if a_buffers == 2 else {"pipeline_mode": pl.Buffered(a_buffers)}
    in_specs = [
        pl.BlockSpec((bm, k), lambda i, j: (i, 0), **a_kwargs),
        pl.BlockSpec((None,) * len(lead) + (k, bn), lambda i, j: lead + (0, j)),
    ]
    args = [a, w]
    for arr, kind, off in extras:
        if kind == "tile":
            in_specs.append(pl.BlockSpec((bm, bn), lambda i, j, off=off: (i, j + off)))
        else:
            in_specs.append(pl.BlockSpec((1, bn), lambda i, j: (0, j)))
        args.append(arr)
    return pl.pallas_call(
        functools.partial(_mm_body, n_extra=len(extras), epilogue=epilogue),
        out_shape=jax.ShapeDtypeStruct((m, n), out_dtype),
        grid=(m // bm, n // bn),
        in_specs=in_specs,
        out_specs=pl.BlockSpec((bm, bn), lambda i, j: (i, j)),
        compiler_params=_params(2),
        name="matmul",
    )(*args)


def _ep_residual(acc, res):
    return res + acc


def _ep_glu_gate(acc, y, b):
    return y * _sigmoid(acc + b)


def _ep_merge(acc, branch_a, g_a, g_b):
    return _sigmoid(g_a) * branch_a + _sigmoid(g_b) * acc


def _gmlp_body(u_ref, v_ref, g_ref, b_ref, w_ref, bias_ref, o_ref, vout_ref, *, heads, hd):
    v = _gelu(v_ref[...])
    mu = jnp.mean(v, axis=-1, keepdims=True)
    vc = v - mu
    var = jnp.mean(vc * vc, axis=-1, keepdims=True)
    vn = vc * lax.rsqrt(var + LN_EPS) * g_ref[...] + b_ref[...]
    vout_ref[...] = vn
    c = vn.shape[0]
    causal = lax.broadcasted_iota(jnp.int32, (c, c), 1) <= lax.broadcasted_iota(jnp.int32, (c, c), 0)
    for h in range(heads):
        sl = slice(h * hd, (h + 1) * hd)
        w = jnp.where(causal, w_ref[0, h], 0.0).astype(BF16)
        mixed = _dot(w, vn[:, sl].astype(BF16)) + bias_ref[0, :, h:h + 1]
        o_ref[:, sl] = (_gelu(u_ref[:, sl]) * mixed).astype(o_ref.dtype)


def _gmlp(proj, ln_g, ln_b, w_s, b_s, *, width, tp, dec_seq):
    t = proj.shape[0]
    heads, chunk, _ = w_s.shape
    hd = width // heads
    assert tp % chunk == 0 and (t - tp) % chunk == 0 and chunk % dec_seq == 0
    rep = chunk // dec_seq
    blk = jnp.arange(chunk) // dec_seq
    same = (blk[:, None] == blk[None, :]).astype(F32)
    w_sample = jnp.tile(w_s[:, :dec_seq, :dec_seq], (1, rep, rep)) * same[None]
    w_eff = jnp.stack([w_s, w_sample])
    bias_eff = jnp.stack([b_s.T, jnp.tile(b_s[:, :dec_seq].T, (rep, 1))])
    n_prompt_chunks = tp // chunk
    variant = lambda c: jnp.where(c >= n_prompt_chunks, 1, 0)
    return pl.pallas_call(
        functools.partial(_gmlp_body, heads=heads, hd=hd),
        out_shape=(jax.ShapeDtypeStruct((t, width), BF16), jax.ShapeDtypeStruct((t, width), F32)),
        grid=(t // chunk,),
        in_specs=[
            pl.BlockSpec((chunk, width), lambda c: (c, 0)),
            pl.BlockSpec((chunk, width), lambda c: (c, 1)),
            pl.BlockSpec((1, width), lambda c: (0, 0)),
            pl.BlockSpec((1, width), lambda c: (0, 0)),
            pl.BlockSpec((1, heads, chunk, chunk), lambda c: (variant(c), 0, 0, 0)),
            pl.BlockSpec((1, chunk, heads), lambda c: (variant(c), 0, 0)),
        ],
        out_specs=(pl.BlockSpec((chunk, width), lambda c: (c, 0)), pl.BlockSpec((chunk, width), lambda c: (c, 0))),
        compiler_params=_params(1),
        name="gmlp_spatial",
    )(proj, proj, ln_g.reshape(1, width), ln_b.reshape(1, width), w_eff, bias_eff)


def _ssm_prep_body(lr_ref, li_ref, ldt_ref, lrc_ref, lic_ref, ldtc_ref, btr_ref, bti_ref, cmr_ref, cmi_ref,
                   ge_ref, kt_ref, f_ref, apr_ref, api_ref, *, ns, ch, state):
    nb = SSM_BLOCK
    w = GROUPS_PER_BUNDLE * ch

    def powers(lr, li, ldt):
        dt = jnp.exp(ldt)
        e = jnp.exp(lr * dt)
        ar = e * jnp.cos(li * dt)
        ai = e * jnp.sin(li * dt)
        ps = [(jnp.ones_like(ar), jnp.zeros_like(ar))]
        for _ in range(nb):
            pr, pi = ps[-1]
            ps.append((pr * ar - pi * ai, pr * ai + pi * ar))
        return ar, ai, ps

    lr, li = lr_ref[0], li_ref[0]
    ar, ai, prow = powers(lr, li, ldt_ref[0])
    _, _, pcol = powers(lrc_ref[0], lic_ref[0], ldtc_ref[0])
    den = lr * lr + li * li
    rr = ((ar - 1.0) * lr + ai * li) / den
    ri = (ai * lr - (ar - 1.0) * li) / den

    mask_b = (lax.broadcasted_iota(jnp.int32, (w, ns), 0) // ch) == (lax.broadcasted_iota(jnp.int32, (w, ns), 1) // state)
    mask_c = (lax.broadcasted_iota(jnp.int32, (ns, w), 0) // state) == (lax.broadcasted_iota(jnp.int32, (ns, w), 1) // ch)
    btr, bti = btr_ref[0], bti_ref[0]
    bbr = jnp.where(mask_b, rr * btr - ri * bti, 0.0)
    bbi = jnp.where(mask_b, rr * bti + ri * btr, 0.0)
    cmr = jnp.where(mask_c, cmr_ref[0], 0.0)
    cmi = jnp.where(mask_c, cmi_ref[0], 0.0)

    kt_ref[...] = jnp.zeros(kt_ref.shape, kt_ref.dtype)
    for d in range(nb):
        pr, pi = prow[d]
        bdr = bbr * pr - bbi * pi
        bdi = bbr * pi + bbi * pr
        i = nb - 1 - d
        ge_ref[0, i * w:(i + 1) * w, 0:ns] = bdr.astype(ge_ref.dtype)
        ge_ref[0, i * w:(i + 1) * w, ns:2 * ns] = bdi.astype(ge_ref.dtype)
        kd = (_dot3(bdr, cmr) - _dot3(bdi, cmi)).astype(kt_ref.dtype)
        for i0 in range(nb - d):
            kt_ref[0, i0 * w:(i0 + 1) * w, (i0 + d) * w:(i0 + d + 1) * w] = kd
    for j in range(nb):
        pr, pi = pcol[j + 1]
        f_ref[0, 0:ns, j * w:(j + 1) * w] = (cmr * pr - cmi * pi).astype(f_ref.dtype)
        f_ref[0, ns:2 * ns, j * w:(j + 1) * w] = (-(cmr * pi + cmi * pr)).astype(f_ref.dtype)
    qr, qi = prow[nb]
    for k in range(apr_ref.shape[1]):
        apr_ref[0, k:k + 1, :] = qr
        api_ref[0, k:k + 1, :] = qi
        qr, qi = qr * qr - qi * qi, 2.0 * qr * qi


def _ssm_prep(a_re, a_im, log_dt, b_re, b_im, c_re, c_im):
    g, state = a_re.shape
    ch = b_re.shape[-1]
    gb = GROUPS_PER_BUNDLE
    assert g % gb == 0 and gb * ch == LANES
    nbun = g // gb
    ns = gb * state
    w = gb * ch
    nb = SSM_BLOCK
    ldt = jnp.repeat(log_dt, state)
    rows = [v.reshape(nbun, 1, ns) for v in (a_re, a_im, ldt)]
    cols = [v.reshape(nbun, ns, 1) for v in (a_re, a_im, ldt)]

    def b_layout(b):
        return jnp.tile(b.transpose(0, 2, 1).reshape(nbun, w, state), (1, 1, gb))

    def c_layout(c):
        cm = c.transpose(0, 2, 1).reshape(nbun, gb, state, ch).transpose(0, 2, 1, 3).reshape(nbun, state, w)
        return jnp.tile(cm, (1, gb, 1))

    spec_row = pl.BlockSpec((1, 1, ns), lambda b: (b, 0, 0))
    spec_col = pl.BlockSpec((1, ns, 1), lambda b: (b, 0, 0))
    spec_b = pl.BlockSpec((1, w, ns), lambda b: (b, 0, 0))
    spec_c = pl.BlockSpec((1, ns, w), lambda b: (b, 0, 0))
    return pl.pallas_call(
        functools.partial(_ssm_prep_body, ns=ns, ch=ch, state=state),
        out_shape=(
            jax.ShapeDtypeStruct((nbun, nb * w, 2 * ns), BF16),
            jax.ShapeDtypeStruct((nbun, nb * w, nb * w), BF16),
            jax.ShapeDtypeStruct((nbun, 2 * ns, nb * w), BF16),
            jax.ShapeDtypeStruct((nbun, 8, ns), F32),
            jax.ShapeDtypeStruct((nbun, 8, ns), F32),
        ),
        grid=(nbun,),
        in_specs=[spec_row] * 3 + [spec_col] * 3 + [spec_b] * 2 + [spec_c] * 2,
        out_specs=(
            pl.BlockSpec((1, nb * w, 2 * ns), lambda b: (b, 0, 0)),
            pl.BlockSpec((1, nb * w, nb * w), lambda b: (b, 0, 0)),
            pl.BlockSpec((1, 2 * ns, nb * w), lambda b: (b, 0, 0)),
            pl.BlockSpec((1, 8, ns), lambda b: (b, 0, 0)),
            pl.BlockSpec((1, 8, ns), lambda b: (b, 0, 0)),
        ),
        compiler_params=_params(1),
        name="ssm_prep",
    )(*rows, *cols, b_layout(b_re), b_layout(b_im), c_layout(c_re), c_layout(c_im))


def _ssm_body(*refs, rows, ns, scan_steps, has_h0):
    if has_h0:
        x_ref, ge_ref, kt_ref, f_ref, apr_ref, api_ref, d_ref, h0r_ref, h0i_ref, y_ref, hr_ref, hi_ref = refs
    else:
        x_ref, ge_ref, kt_ref, f_ref, apr_ref, api_ref, d_ref, y_ref, hr_ref, hi_ref = refs
    nb = SSM_BLOCK
    us = [x_ref[pl.ds(j, rows, stride=nb), :] for j in range(nb)]
    u = jnp.concatenate(us, axis=1).astype(BF16)
    e = _dot(u, ge_ref[0])
    cr, ci = e[:, :ns], e[:, ns:]
    if has_h0:
        pr, pi = h0r_ref[...], h0i_ref[...]
        ar, ai = apr_ref[0, 0:1, :], api_ref[0, 0:1, :]
        cr = ar * pr - ai * pi + cr
        ci = ar * pi + ai * pr + ci
        hr_ref[...] = cr
        hi_ref[...] = ci
    else:
        row = lax.broadcasted_iota(jnp.int32, (rows, 1), 0)
        for k in range(scan_steps):
            sh = 1 << k
            keep = row >= sh
            sr = jnp.where(keep, pltpu.roll(cr, sh, 0), 0.0)
            si = jnp.where(keep, pltpu.roll(ci, sh, 0), 0.0)
            ar, ai = apr_ref[0, k:k + 1, :], api_ref[0, k:k + 1, :]
            cr, ci = cr + ar * sr - ai * si, ci + ar * si + ai * sr
        keep = row >= 1
        pr = jnp.where(keep, pltpu.roll(cr, 1, 0), 0.0)
        pi = jnp.where(keep, pltpu.roll(ci, 1, 0), 0.0)
        hr_ref[...] = cr[rows - 1:rows, :]
        hi_ref[...] = ci[rows - 1:rows, :]
    carry = jnp.concatenate([pr, pi], axis=1).astype(BF16)
    y = _dot(u, kt_ref[0]) + _dot(carry, f_ref[0])
    d = d_ref[...]
    w = d.shape[-1]
    for j in range(nb):
        y_ref[pl.ds(j, rows, stride=nb), :] = _gelu(y[:, j * w:(j + 1) * w] + d * us[j])


def _ssm_scan(proj, mats, d_skip, *, col_block, row_start, n_seq, seq_len, h0=None):
    ge, kt, f, apr, api = mats
    nbun, _, ns2 = ge.shape
    ns = ns2 // 2
    w = LANES
    nb = SSM_BLOCK
    width = nbun * w
    has_h0 = h0 is not None
    if has_h0:
        assert seq_len == nb
        rows = n_seq
        steps, tok = 1, n_seq * nb
        scan_steps = 0
    else:
        rows = seq_len // nb
        steps, tok = n_seq, seq_len
        scan_steps = int(math.log2(rows))
        assert 1 << scan_steps == rows and scan_steps <= apr.shape[1]
    assert row_start % tok == 0
    r0 = row_start // tok
    mat_specs = [
        pl.BlockSpec((1,) + ge.shape[1:], lambda b, s: (b, 0, 0)),
        pl.BlockSpec((1,) + kt.shape[1:], lambda b, s: (b, 0, 0)),
        pl.BlockSpec((1,) + f.shape[1:], lambda b, s: (b, 0, 0)),
        pl.BlockSpec((1,) + apr.shape[1:], lambda b, s: (b, 0, 0)),
        pl.BlockSpec((1,) + api.shape[1:], lambda b, s: (b, 0, 0)),
        pl.BlockSpec((1, w), lambda b, s: (0, b)),
    ]
    in_specs = [pl.BlockSpec((tok, w), lambda b, s: (r0 + s, col_block + b))] + mat_specs
    args = [proj, ge, kt, f, apr, api, d_skip.reshape(1, width)]
    if has_h0:
        in_specs += [pl.BlockSpec((rows, ns), lambda b, s: (0, b))] * 2
        args += [h0[0], h0[1]]
        state_shape = jax.ShapeDtypeStruct((rows, nbun * ns), F32)
        state_spec = pl.BlockSpec((rows, ns), lambda b, s: (0, b))
    else:
        state_shape = jax.ShapeDtypeStruct((n_seq, 1, nbun * ns), F32)
        state_spec = pl.BlockSpec((None, 1, ns), lambda b, s: (s, 0, b))
    y, hr, hi = pl.pallas_call(
        functools.partial(_ssm_body, rows=rows, ns=ns, scan_steps=scan_steps, has_h0=has_h0),
        out_shape=(jax.ShapeDtypeStruct((steps * tok, width), F32), state_shape, state_shape),
        grid=(nbun, steps),
        in_specs=in_specs,
        out_specs=(pl.BlockSpec((tok, w), lambda b, s: (s, b)), state_spec, state_spec),
        compiler_params=_params(2),
        name="ssm_scan",
    )(*args)
    return y, hr.reshape(n_seq, nbun * ns), hi.reshape(n_seq, nbun * ns)


def _glu_up_body(te_ref, nv_ref, a_ref, wg_ref, wu_ref, o_ref):
    valid = pl.program_id(1) < nv_ref[0]

    @pl.when(valid)
    def _():
        a = a_ref[...]
        g = _dot(a, wg_ref[...].astype(BF16))
        u = _dot(a, wu_ref[...].astype(BF16))
        o_ref[...] = (g * _sigmoid(g) * u).astype(o_ref.dtype)

    @pl.when(jnp.logical_not(valid))
    def _():
        o_ref[...] = jnp.zeros(o_ref.shape, o_ref.dtype)


def _glu_up(a, w_gate, w_up, lead, tile_expert, n_valid, *, bm, bn):
    m, k = a.shape
    n = w_gate.shape[-1]
    nt = m // bm
    bn = _tile(n, bn)
    lead = tuple(lead)

    def row(i, nv):
        return jnp.minimum(i, nv[0] - 1)

    w_spec = pl.BlockSpec((None,) * (len(lead) + 1) + (k, bn), lambda j, i, te, nv: lead + (te[row(i, nv)], 0, j))
    return pl.pallas_call(
        _glu_up_body,
        out_shape=jax.ShapeDtypeStruct((m, n), BF16),
        grid_spec=pltpu.PrefetchScalarGridSpec(
            num_scalar_prefetch=2,
            grid=(n // bn, nt),
            in_specs=[pl.BlockSpec((bm, k), lambda j, i, te, nv: (row(i, nv), 0)), w_spec, w_spec],
            out_specs=pl.BlockSpec((bm, bn), lambda j, i, te, nv: (i, j)),
        ),
        compiler_params=_params(2),
        name="swiglu_up",
    )(tile_expert, n_valid, a, w_gate, w_up)


def _glu_down_body(te_ref, nv_ref, h_ref, w_ref, o_ref):
    k = pl.program_id(1)
    valid = pl.program_id(0) < nv_ref[0]

    @pl.when(jnp.logical_and(jnp.logical_not(valid), k == 0))
    def _():
        o_ref[...] = jnp.zeros(o_ref.shape, o_ref.dtype)

    @pl.when(valid)
    def _():
        p = _dot(h_ref[...], w_ref[...].astype(BF16))

        @pl.when(k == 0)
        def _():
            o_ref[...] = p

        @pl.when(k > 0)
        def _():
            o_ref[...] += p


def _glu_down(h, w_down, lead, tile_expert, n_valid, *, bm, bk):
    m, kdim = h.shape
    n = w_down.shape[-1]
    nt = m // bm
    bk = _tile(kdim, bk)
    nk = kdim // bk
    lead = tuple(lead)

    def row(i, nv):
        return jnp.minimum(i, nv[0] - 1)

    def kk(i, k, nv):
        return jnp.where(i < nv[0], k, nk - 1)

    return pl.pallas_call(
        _glu_down_body,
        out_shape=jax.ShapeDtypeStruct((m, n), F32),
        grid_spec=pltpu.PrefetchScalarGridSpec(
            num_scalar_prefetch=2,
            grid=(nt, nk),
            in_specs=[
                pl.BlockSpec((bm, bk), lambda i, k, te, nv: (row(i, nv), kk(i, k, nv))),
                pl.BlockSpec((None,) * (len(lead) + 1) + (bk, n),
                             lambda i, k, te, nv: lead + (te[row(i, nv)], kk(i, k, nv), 0)),
            ],
            out_specs=pl.BlockSpec((bm, n), lambda i, k, te, nv: (i, 0)),
        ),
        compiler_params=_params(2),
        name="swiglu_down",
    )(tile_expert, n_valid, h, w_down)


def _router_body(x_ref, g_ref, wr_ref, xn_ref, route_ref, *, n_experts):
    x = x_ref[...]
    xn = x * lax.rsqrt(jnp.mean(x * x, axis=-1, keepdims=True) + RMS_EPS) * g_ref[...]
    xn_ref[...] = xn.astype(xn_ref.dtype)
    logits = _dot3(xn, wr_ref[...])
    lane = lax.broadcasted_iota(jnp.int32, logits.shape, 1)
    valid = lane < n_experts
    lg = jnp.where(valid, logits, -jnp.inf)
    ex = jnp.where(valid, jnp.exp(lg - jnp.max(lg, axis=-1, keepdims=True)), 0.0)
    probs = ex / jnp.sum(ex, axis=-1, keepdims=True)
    big = logits.shape[-1]
    p1 = jnp.max(probs, axis=-1, keepdims=True)
    i1 = jnp.min(jnp.where(probs == p1, lane, big), axis=-1, keepdims=True)
    rest = jnp.where(lane == i1, -1.0, jnp.where(valid, probs, -1.0))
    p2 = jnp.max(rest, axis=-1, keepdims=True)
    i2 = jnp.min(jnp.where(rest == p2, lane, big), axis=-1, keepdims=True)
    s = p1 + p2
    out = jnp.where(lane == 0, i1.astype(F32), 0.0)
    out = jnp.where(lane == 1, i2.astype(F32), out)
    out = jnp.where(lane == 2, p1 / s, out)
    out = jnp.where(lane == 3, p2 / s, out)
    route_ref[...] = out


def _router(x, g, w_router):
    t, d = x.shape
    n_experts = w_router.shape[-1]
    wr = jnp.pad(w_router, ((0, 0), (0, LANES - n_experts)))
    bm = _tile(t, 256, 8)
    return pl.pallas_call(
        functools.partial(_router_body, n_experts=n_experts),
        out_shape=(jax.ShapeDtypeStruct((t, d), BF16), jax.ShapeDtypeStruct((t, LANES), F32)),
        grid=(t // bm,),
        in_specs=[pl.BlockSpec((bm, d), lambda i: (i, 0)), pl.BlockSpec((1, d), lambda i: (0, 0)),
                  pl.BlockSpec((d, LANES), lambda i: (0, 0))],
        out_specs=(pl.BlockSpec((bm, d), lambda i: (i, 0)), pl.BlockSpec((bm, LANES), lambda i: (i, 0))),
        compiler_params=_params(1),
        name="moe_router",
    )(x, g.reshape(1, d), wr)


def _row_copy(src_hbm, src_row, dst_ref, dst_row, sem):
    return pltpu.make_async_copy(src_hbm.at[pl.ds(src_row, 1), :], dst_ref.at[pl.ds(dst_row, 1), :], sem)


def _gather_body(idx_ref, x_hbm, o_ref, sem, *, bm):
    def start(r, c):
        _row_copy(x_hbm, idx_ref[0, 0, r], o_ref, r, sem).start()
        return c

    def wait(r, c):
        _row_copy(x_hbm, 0, o_ref, r, sem).wait()
        return c

    lax.fori_loop(0, bm, start, 0)
    lax.fori_loop(0, bm, wait, 0)


def _gather_rows(x, idx, *, bm):
    n = idx.shape[0]
    d = x.shape[1]
    nt = n // bm
    return pl.pallas_call(
        functools.partial(_gather_body, bm=bm),
        out_shape=jax.ShapeDtypeStruct((n, d), x.dtype),
        grid=(nt,),
        in_specs=[pl.BlockSpec((1, 1, bm), lambda i: (i, 0, 0), memory_space=pltpu.SMEM),
                  pl.BlockSpec(memory_space=pl.ANY)],
        out_specs=pl.BlockSpec((bm, d), lambda i: (i, 0)),
        scratch_shapes=[pltpu.SemaphoreType.DMA],
        compiler_params=_params(1),
        name="moe_gather",
    )(idx.reshape(nt, 1, bm), x)


def _combine_body(d1_ref, d2_ref, x_ref, route_ref, g_ref, ys_hbm, o_ref, buf1, buf2, sem, *, bm, final_norm):
    def start(r, c):
        _row_copy(ys_hbm, d1_ref[0, 0, r], buf1, r, sem).start()
        _row_copy(ys_hbm, d2_ref[0, 0, r], buf2, r, sem).start()
        return c

    def wait(r, c):
        _row_copy(ys_hbm, 0, buf1, r, sem).wait()
        _row_copy(ys_hbm, 0, buf2, r, sem).wait()
        return c

    lax.fori_loop(0, bm, start, 0)
    lax.fori_loop(0, bm, wait, 0)
    route = route_ref[...]
    x = x_ref[...] + route[:, 2:3] * buf1[...] + route[:, 3:4] * buf2[...]
    if final_norm:
        x = x * lax.rsqrt(jnp.mean(x * x, axis=-1, keepdims=True) + RMS_EPS) * g_ref[...]
    o_ref[...] = x


def _combine(x, route, ys, d1, d2, g, *, final_norm, bm=128):
    t, d = x.shape
    bm = _tile(t, bm, 8)
    nt = t // bm
    idx_spec = pl.BlockSpec((1, 1, bm), lambda i: (i, 0, 0), memory_space=pltpu.SMEM)
    return pl.pallas_call(
        functools.partial(_combine_body, bm=bm, final_norm=final_norm),
        out_shape=jax.ShapeDtypeStruct((t, d), F32),
        grid=(nt,),
        in_specs=[idx_spec, idx_spec,
                  pl.BlockSpec((bm, d), lambda i: (i, 0)),
                  pl.BlockSpec((bm, LANES), lambda i: (i, 0)),
                  pl.BlockSpec((1, d), lambda i: (0, 0)),
                  pl.BlockSpec(memory_space=pl.ANY)],
        out_specs=pl.BlockSpec((bm, d), lambda i: (i, 0)),
        scratch_shapes=[pltpu.VMEM((bm, d), F32), pltpu.VMEM((bm, d), F32), pltpu.SemaphoreType.DMA],
        compiler_params=_params(1),
        name="moe_combine",
    )(d1.reshape(nt, 1, bm), d2.reshape(nt, 1, bm), x, route, g.reshape(1, d), ys)


def _moe(x, g, w_router, w_gate, w_up, w_down, lead, final_gain, *, bm=512):
    t, d = x.shape
    n_experts = w_router.shape[-1]
    xn, route = _router(x, g, w_router)
    eid = route[:, :TOP_K].astype(jnp.int32).reshape(-1)
    onehot = (eid[:, None] == jnp.arange(n_experts)[None, :]).astype(jnp.int32)
    rank = jnp.sum((jnp.cumsum(onehot, axis=0) - onehot) * onehot, axis=1)
    tiles_per = (jnp.sum(onehot, axis=0) + bm - 1) // bm
    tile_end = jnp.cumsum(tiles_per)
    dest = (tile_end - tiles_per)[eid] * bm + rank
    n_tiles = (TOP_K * t) // bm + n_experts
    n_valid = tile_end[-1:].astype(jnp.int32)
    tile_expert = jnp.minimum(jnp.searchsorted(tile_end, jnp.arange(n_tiles), side="right"),
                              n_experts - 1).astype(jnp.int32)
    token = jnp.arange(TOP_K * t, dtype=jnp.int32) // TOP_K
    src = jnp.zeros((n_tiles * bm,), jnp.int32).at[dest].set(token)
    xn32 = lax.bitcast_convert_type(xn.reshape(t, d // 2, 2), jnp.uint32)
    xs = lax.bitcast_convert_type(_gather_rows(xn32, src, bm=bm), BF16).reshape(n_tiles * bm, d)
    h = _glu_up(xs, w_gate, w_up, lead, tile_expert, n_valid, bm=bm, bn=512)
    ys = _glu_down(h, w_down, lead, tile_expert, n_valid, bm=bm, bk=512)
    dest2 = dest.reshape(t, TOP_K).astype(jnp.int32)
    gain = final_gain if final_gain is not None else jnp.ones((d,), F32)
    return _combine(x, route, ys, dest2[:, 0], dest2[:, 1], gain, final_norm=final_gain is not None)


def kernel(x_prompt, x_sample, state_ssm_re, state_ssm_im, norm_mix, norm_ffn, norm_final,
           w_in, gmlp_ln_g, gmlp_ln_b, gmlp_w_s, gmlp_b_s, w_proj_a,
           ssm_a_re, ssm_a_im, ssm_log_dt, ssm_b_re, ssm_b_im, ssm_c_re, ssm_c_im, ssm_d,
           w_glu, b_glu, w_proj_b, w_out,
           ffn_w_gate, ffn_w_up, ffn_w_down,
           moe_router, moe_w_gate, moe_w_up, moe_w_down):
    nb_, seq, d = x_prompt.shape
    db, ds, _ = x_sample.shape
    tp, ts = nb_ * seq, db * ds
    t = tp + ts
    depth = w_in.shape[0]
    gw = gmlp_ln_g.shape[1]
    sw = ssm_d.shape[1]
    groups, state = ssm_a_re.shape[1:]
    assert ds == SSM_BLOCK and gw % LANES == 0 and sw % LANES == 0 and d % LANES == 0
    xb_col = (2 * gw) // LANES
    x = jnp.concatenate([x_prompt.reshape(tp, d), x_sample.reshape(ts, d)], axis=0)

    hre_p, him_p, hre_s, him_s, v_rows = [], [], [], [], []
    for l in range(depth):
        xn = _rmsnorm(x, norm_mix[l], BF16)
        proj = _matmul(xn, w_in, (l,), F32)
        gated, v_all = _gmlp(proj, gmlp_ln_g[l], gmlp_ln_b[l], gmlp_w_s[l], gmlp_b_s[l], width=gw, tp=tp, dec_seq=ds)
        v_rows.append(v_all[tp:].reshape(db, ds, gw))
        branch_a = _matmul(gated, w_proj_a, (l,), F32)
        mats = _ssm_prep(ssm_a_re[l], ssm_a_im[l], ssm_log_dt[l], ssm_b_re[l], ssm_b_im[l], ssm_c_re[l], ssm_c_im[l])
        y_p, hr, hi = _ssm_scan(proj, mats, ssm_d[l], col_block=xb_col, row_start=0, n_seq=nb_, seq_len=seq)
        hre_p.append(hr.reshape(nb_, groups, state))
        him_p.append(hi.reshape(nb_, groups, state))
        h0 = (state_ssm_re[l].reshape(db, groups * state), state_ssm_im[l].reshape(db, groups * state))
        y_s, hr, hi = _ssm_scan(proj, mats, ssm_d[l], col_block=xb_col, row_start=tp, n_seq=db, seq_len=ds, h0=h0)
        hre_s.append(hr.reshape(db, groups, state))
        him_s.append(hi.reshape(db, groups, state))
        yg = jnp.concatenate([y_p, y_s], axis=0)
        bn_glu = _tile(sw, 512)
        y_glu = _matmul(yg, w_glu, (l,), BF16, bn=bn_glu, epilogue=_ep_glu_gate,
                        extras=[(yg, "tile", 0), (b_glu[l].reshape(1, sw), "row", 0)])
        bn_m = math.gcd(_tile(d, 512), 2 * gw + sw, 2 * gw + sw + d)
        merged = _matmul(y_glu, w_proj_b, (l,), BF16, bn=bn_m, epilogue=_ep_merge,
                         extras=[(branch_a, "tile", 0), (proj, "tile", (2 * gw + sw) // bn_m),
                                 (proj, "tile", (2 * gw + sw + d) // bn_m)])
        x = _matmul(merged, w_out, (l,), F32, epilogue=_ep_residual, extras=[(x, "tile", 0)])
        j = l // 2
        last = l == depth - 1
        if l % 2 == 0:
            xn = _rmsnorm(x, norm_ffn[l], BF16)
            bm = _tile(t, 512, 8)
            nt = t // bm
            h = _glu_up(xn, ffn_w_gate, ffn_w_up, (), jnp.full((nt,), j, jnp.int32), jnp.full((1,), nt, jnp.int32),
                        bm=bm, bn=256)
            x = _matmul(h, ffn_w_down, (j,), F32, bm=512, bn=256, epilogue=_ep_residual, extras=[(x, "tile", 0)],
                        a_buffers=1)
            if last:
                x = _rmsnorm(x, norm_final, F32)
        else:
            x = _moe(x, norm_ffn[l], moe_router[j], moe_w_gate, moe_w_up, moe_w_down, (j,),
                     norm_final if last else None)
    y = x
    return (y[:tp].reshape(nb_, seq, d), y[tp:].reshape(db, ds, d),
            jnp.stack(hre_p), jnp.stack(him_p), jnp.stack(hre_s), jnp.stack(him_s), jnp.stack(v_rows))
```

```python
import functools
import math

import jax
import jax.numpy as jnp
from jax import lax
from jax.experimental import pallas as pl
from jax.experimental.pallas import tpu as pltpu

F32 = jnp.float32
BF16 = jnp.bfloat16

RMS_EPS = 1e-5
LN_EPS = 1e-5
TOP_K = 2

LANES = 128
SSM_BLOCK = 8
GROUPS_PER_BUNDLE = 8
V7X_VMEM_LIMIT = 56 * 1024 * 1024


def _params(n_axes, vmem=V7X_VMEM_LIMIT):
    return pltpu.CompilerParams(dimension_semantics=("arbitrary",) * n_axes, vmem_limit_bytes=vmem)


def _tile(n, pref, mult=LANES):
    if n <= pref:
        return n
    t = (pref // mult) * mult
    while t >= mult:
        if n % t == 0:
            return t
        t -= mult
    raise ValueError(f"no tile for {n} <= {pref}")


def _gelu(x):
    return 0.5 * x * (1.0 + jnp.tanh(0.7978845608028654 * (x + 0.044715 * (x * x * x))))


def _sigmoid(x):
    return 1.0 / (1.0 + jnp.exp(-x))


def _dot(a, b):
    return jnp.dot(a, b, preferred_element_type=F32)


def _dot3(a, b):
    ah = a.astype(BF16)
    al = (a - ah.astype(F32)).astype(BF16)
    bh = b.astype(BF16)
    bl = (b - bh.astype(F32)).astype(BF16)
    return _dot(ah, bh) + _dot(ah, bl) + _dot(al, bh)


def _rmsnorm_body(x_ref, g_ref, o_ref):
    x = x_ref[...]
    inv = lax.rsqrt(jnp.mean(x * x, axis=-1, keepdims=True) + RMS_EPS)
    o_ref[...] = (x * inv * g_ref[...]).astype(o_ref.dtype)


def _rmsnorm(x, g, out_dtype):
    t, d = x.shape
    bm = _tile(t, 256, 8)
    return pl.pallas_call(
        _rmsnorm_body,
        out_shape=jax.ShapeDtypeStruct((t, d), out_dtype),
        grid=(t // bm,),
        in_specs=[pl.BlockSpec((bm, d), lambda i: (i, 0)), pl.BlockSpec((1, d), lambda i: (0, 0))],
        out_specs=pl.BlockSpec((bm, d), lambda i: (i, 0)),
        compiler_params=_params(1),
        name="rmsnorm",
    )(x, g.reshape(1, d))


def _mm_body(*refs, n_extra, epilogue):
    a_ref, w_ref = refs[0], refs[1]
    extras = refs[2:2 + n_extra]
    o_ref = refs[2 + n_extra]
    acc = _dot(a_ref[...].astype(BF16), w_ref[...].astype(BF16))
    if epilogue is not None:
        acc = epilogue(acc, *[r[...] for r in extras])
    o_ref[...] = acc.astype(o_ref.dtype)


def _matmul(a, w, lead, out_dtype, *, bm=1024, bn=512, epilogue=None, extras=(), a_buffers=2):
    m, k = a.shape
    n = w.shape[-1]
    assert w.shape[-2] == k
    bm = _tile(m, bm, 8)
    bn = _tile(n, bn)
    lead = tuple(lead)
    a_kwargs = {} if a_buffers == 2 else {"pipeline_mode": pl.Buffered(a_buffers)}
    in_specs = [
        pl.BlockSpec((bm, k), lambda i, j: (i, 0), **a_kwargs),
        pl.BlockSpec((None,) * len(lead) + (k, bn), lambda i, j: lead + (0, j)),
    ]
    args = [a, w]
    for arr, kind, off in extras:
        if kind == "tile":
            in_specs.append(pl.BlockSpec((bm, bn), lambda i, j, off=off: (i, j + off)))
        else:
            in_specs.append(pl.BlockSpec((1, bn), lambda i, j: (0, j)))
        args.append(arr)
    return pl.pallas_call(
        functools.partial(_mm_body, n_extra=len(extras), epilogue=epilogue),
        out_shape=jax.ShapeDtypeStruct((m, n), out_dtype),
        grid=(m // bm, n // bn),
        in_specs=in_specs,
        out_specs=pl.BlockSpec((bm, bn), lambda i, j: (i, j)),
        compiler_params=_params(2),
        name="matmul",
    )(*args)


def _ep_residual(acc, res):
    return res + acc


def _ep_glu_gate(acc, y, b):
    return y * _sigmoid(acc + b)


def _ep_merge(acc, branch_a, g_a, g_b):
    return _sigmoid(g_a) * branch_a + _sigmoid(g_b) * acc


def _gmlp_body(u_ref, v_ref, g_ref, b_ref, w_ref, bias_ref, o_ref, vout_ref, *, heads, hd):
    v = _gelu(v_ref[...])
    mu = jnp.mean(v, axis=-1, keepdims=True)
    vc = v - mu
    var = jnp.mean(vc * vc, axis=-1, keepdims=True)
    vn = vc * lax.rsqrt(var + LN_EPS) * g_ref[...] + b_ref[...]
    vout_ref[...] = vn
    c = vn.shape[0]
    causal = lax.broadcasted_iota(jnp.int32, (c, c), 1) <= lax.broadcasted_iota(jnp.int32, (c, c), 0)
    for h in range(heads):
        sl = slice(h * hd, (h + 1) * hd)
        w = jnp.where(causal, w_ref[0, h], 0.0).astype(BF16)
        mixed = _dot(w, vn[:, sl].astype(BF16)) + bias_ref[0, :, h:h + 1]
        o_ref[:, sl] = (_gelu(u_ref[:, sl]) * mixed).astype(o_ref.dtype)


def _gmlp(proj, ln_g, ln_b, w_s, b_s, *, width, tp, dec_seq):
    t = proj.shape[0]
    heads, chunk, _ = w_s.shape
    hd = width // heads
    assert tp % chunk == 0 and (t - tp) % chunk == 0 and chunk % dec_seq == 0
    rep = chunk // dec_seq
    blk = jnp.arange(chunk) // dec_seq
    same = (blk[:, None] == blk[None, :]).astype(F32)
    w_sample = jnp.tile(w_s[:, :dec_seq, :dec_seq], (1, rep, rep)) * same[None]
    w_eff = jnp.stack([w_s, w_sample])
    bias_eff = jnp.stack([b_s.T, jnp.tile(b_s[:, :dec_seq].T, (rep, 1))])
    n_prompt_chunks = tp // chunk
    variant = lambda c: jnp.where(c >= n_prompt_chunks, 1, 0)
    return pl.pallas_call(
        functools.partial(_gmlp_body, heads=heads, hd=hd),
        out_shape=(jax.ShapeDtypeStruct((t, width), BF16), jax.ShapeDtypeStruct((t, width), F32)),
        grid=(t // chunk,),
        in_specs=[
            pl.BlockSpec((chunk, width), lambda c: (c, 0)),
            pl.BlockSpec((chunk, width), lambda c: (c, 1)),
            pl.BlockSpec((1, width), lambda c: (0, 0)),
            pl.BlockSpec((1, width), lambda c: (0, 0)),
            pl.BlockSpec((1, heads, chunk, chunk), lambda c: (variant(c), 0, 0, 0)),
            pl.BlockSpec((1, chunk, heads), lambda c: (variant(c), 0, 0)),
        ],
        out_specs=(pl.BlockSpec((chunk, width), lambda c: (c, 0)), pl.BlockSpec((chunk, width), lambda c: (c, 0))),
        compiler_params=_params(1),
        name="gmlp_spatial",
    )(proj, proj, ln_g.reshape(1, width), ln_b.reshape(1, width), w_eff, bias_eff)


def _ssm_prep_body(lr_ref, li_ref, ldt_ref, lrc_ref, lic_ref, ldtc_ref, btr_ref, bti_ref, cmr_ref, cmi_ref,
                   ge_ref, kt_ref, f_ref, apr_ref, api_ref, *, ns, ch, state):
    nb = SSM_BLOCK
    w = GROUPS_PER_BUNDLE * ch

    def powers(lr, li, ldt):
        dt = jnp.exp(ldt)
        e = jnp.exp(lr * dt)
        ar = e * jnp.cos(li * dt)
        ai = e * jnp.sin(li * dt)
        ps = [(jnp.ones_like(ar), jnp.zeros_like(ar))]
        for _ in range(nb):
            pr, pi = ps[-1]
            ps.append((pr * ar - pi * ai, pr * ai + pi * ar))
        return ar, ai, ps

    lr, li = lr_ref[0], li_ref[0]
    ar, ai, prow = powers(lr, li, ldt_ref[0])
    _, _, pcol = powers(lrc_ref[0], lic_ref[0], ldtc_ref[0])
    den = lr * lr + li * li
    rr = ((ar - 1.0) * lr + ai * li) / den
    ri = (ai * lr - (ar - 1.0) * li) / den

    mask_b = (lax.broadcasted_iota(jnp.int32, (w, ns), 0) // ch) == (lax.broadcasted_iota(jnp.int32, (w, ns), 1) // state)
    mask_c = (lax.broadcasted_iota(jnp.int32, (ns, w), 0) // state) == (lax.broadcasted_iota(jnp.int32, (ns, w), 1) // ch)
    btr, bti = btr_ref[0], bti_ref[0]
    bbr = jnp.where(mask_b, rr * btr - ri * bti, 0.0)
    bbi = jnp.where(mask_b, rr * bti + ri * btr, 0.0)
    cmr = jnp.where(mask_c, cmr_ref[0], 0.0)
    cmi = jnp.where(mask_c, cmi_ref[0], 0.0)

    kt_ref[...] = jnp.zeros(kt_ref.shape, kt_ref.dtype)
    for d in range(nb):
        pr, pi = prow[d]
        bdr = bbr * pr - bbi * pi
        bdi = bbr * pi + bbi * pr
        i = nb - 1 - d
        ge_ref[0, i * w:(i + 1) * w, 0:ns] = bdr.astype(ge_ref.dtype)
        ge_ref[0, i * w:(i + 1) * w, ns:2 * ns] = bdi.astype(ge_ref.dtype)
        kd = (_dot3(bdr, cmr) - _dot3(bdi, cmi)).astype(kt_ref.dtype)
        for i0 in range(nb - d):
            kt_ref[0, i0 * w:(i0 + 1) * w, (i0 + d) * w:(i0 + d + 1) * w] = kd
    for j in range(nb):
        pr, pi = pcol[j + 1]
        f_ref[0, 0:ns, j * w:(j + 1) * w] = (cmr * pr - cmi * pi).astype(f_ref.dtype)
        f_ref[0, ns:2 * ns, j * w:(j + 1) * w] = (-(cmr * pi + cmi * pr)).astype(f_ref.dtype)
    qr, qi = prow[nb]
    for k in range(apr_ref.shape[1]):
        apr_ref[0, k:k + 1, :] = qr
        api_ref[0, k:k + 1, :] = qi
        qr, qi = qr * qr - qi * qi, 2.0 * qr * qi


def _ssm_prep(a_re, a_im, log_dt, b_re, b_im, c_re, c_im):
    g, state = a_re.shape
    ch = b_re.shape[-1]
    gb = GROUPS_PER_BUNDLE
    assert g % gb == 0 and gb * ch == LANES
    nbun = g // gb
    ns = gb * state
    w = gb * ch
    nb = SSM_BLOCK
    ldt = jnp.repeat(log_dt, state)
    rows = [v.reshape(nbun, 1, ns) for v in (a_re, a_im, ldt)]
    cols = [v.reshape(nbun, ns, 1) for v in (a_re, a_im, ldt)]

    def b_layout(b):
        return jnp.tile(b.transpose(0, 2, 1).reshape(nbun, w, state), (1, 1, gb))

    def c_layout(c):
        cm = c.transpose(0, 2, 1).reshape(nbun, gb, state, ch).transpose(0, 2, 1, 3).reshape(nbun, state, w)
        return jnp.tile(cm, (1, gb, 1))

    spec_row = pl.BlockSpec((1, 1, ns), lambda b: (b, 0, 0))
    spec_col = pl.BlockSpec((1, ns, 1), lambda b: (b, 0, 0))
    spec_b = pl.BlockSpec((1, w, ns), lambda b: (b, 0, 0))
    spec_c = pl.BlockSpec((1, ns, w), lambda b: (b, 0, 0))
    return pl.pallas_call(
        functools.partial(_ssm_prep_body, ns=ns, ch=ch, state=state),
        out_shape=(
            jax.ShapeDtypeStruct((nbun, nb * w, 2 * ns), BF16),
            jax.ShapeDtypeStruct((nbun, nb * w, nb * w), BF16),
            jax.ShapeDtypeStruct((nbun, 2 * ns, nb * w), BF16),
            jax.ShapeDtypeStruct((nbun, 8, ns), F32),
            jax.ShapeDtypeStruct((nbun, 8, ns), F32),
        ),
        grid=(nbun,),
        in_specs=[spec_row] * 3 + [spec_col] * 3 + [spec_b] * 2 + [spec_c] * 2,
        out_specs=(
            pl.BlockSpec((1, nb * w, 2 * ns), lambda b: (b, 0, 0)),
            pl.BlockSpec((1, nb * w, nb * w), lambda b: (b, 0, 0)),
            pl.BlockSpec((1, 2 * ns, nb * w), lambda b: (b, 0, 0)),
            pl.BlockSpec((1, 8, ns), lambda b: (b, 0, 0)),
            pl.BlockSpec((1, 8, ns), lambda b: (b, 0, 0)),
        ),
        compiler_params=_params(1),
        name="ssm_prep",
    )(*rows, *cols, b_layout(b_re), b_layout(b_im), c_layout(c_re), c_layout(c_im))


def _ssm_body(*refs, rows, ns, scan_steps, has_h0):
    if has_h0:
        x_ref, ge_ref, kt_ref, f_ref, apr_ref, api_ref, d_ref, h0r_ref, h0i_ref, y_ref, hr_ref, hi_ref = refs
    else:
        x_ref, ge_ref, kt_ref, f_ref, apr_ref, api_ref, d_ref, y_ref, hr_ref, hi_ref = refs
    nb = SSM_BLOCK
    us = [x_ref[pl.ds(j, rows, stride=nb), :] for j in range(nb)]
    u = jnp.concatenate(us, axis=1).astype(BF16)
    e = _dot(u, ge_ref[0])
    cr, ci = e[:, :ns], e[:, ns:]
    if has_h0:
        pr, pi = h0r_ref[...], h0i_ref[...]
        ar, ai = apr_ref[0, 0:1, :], api_ref[0, 0:1, :]
        cr = ar * pr - ai * pi + cr
        ci = ar * pi + ai * pr + ci
        hr_ref[...] = cr
        hi_ref[...] = ci
    else:
        row = lax.broadcasted_iota(jnp.int32, (rows, 1), 0)
        for k in range(scan_steps):
            sh = 1 << k
            keep = row >= sh
            sr = jnp.where(keep, pltpu.roll(cr, sh, 0), 0.0)
            si = jnp.where(keep, pltpu.roll(ci, sh, 0), 0.0)
            ar, ai = apr_ref[0, k:k + 1, :], api_ref[0, k:k + 1, :]
            cr, ci = cr + ar * sr - ai * si, ci + ar * si + ai * sr
        keep = row >= 1
        pr = jnp.where(keep, pltpu.roll(cr, 1, 0), 0.0)
        pi = jnp.where(keep, pltpu.roll(ci, 1, 0), 0.0)
        hr_ref[...] = cr[rows - 1:rows, :]
        hi_ref[...] = ci[rows - 1:rows, :]
    carry = jnp.concatenate([pr, pi], axis=1).astype(BF16)
    y = _dot(u, kt_ref[0]) + _dot(carry, f_ref[0])
    d = d_ref[...]
    w = d.shape[-1]
    for j in range(nb):
        y_ref[pl.ds(j, rows, stride=nb), :] = _gelu(y[:, j * w:(j + 1) * w] + d * us[j])


def _ssm_scan(proj, mats, d_skip, *, col_block, row_start, n_seq, seq_len, h0=None):
    ge, kt, f, apr, api = mats
    nbun, _, ns2 = ge.shape
    ns = ns2 // 2
    w = LANES
    nb = SSM_BLOCK
    width = nbun * w
    has_h0 = h0 is not None
    if has_h0:
        assert seq_len == nb
        rows = n_seq
        steps, tok = 1, n_seq * nb
        scan_steps = 0
    else:
        rows = seq_len // nb
        steps, tok = n_seq, seq_len
        scan_steps = int(math.log2(rows))
        assert 1 << scan_steps == rows and scan_steps <= apr.shape[1]
    assert row_start % tok == 0
    r0 = row_start // tok
    mat_specs = [
        pl.BlockSpec((1,) + ge.shape[1:], lambda b, s: (b, 0, 0)),
        pl.BlockSpec((1,) + kt.shape[1:], lambda b, s: (b, 0, 0)),
        pl.BlockSpec((1,) + f.shape[1:], lambda b, s: (b, 0, 0)),
        pl.BlockSpec((1,) + apr.shape[1:], lambda b, s: (b, 0, 0)),
        pl.BlockSpec((1,) + api.shape[1:], lambda b, s: (b, 0, 0)),
        pl.BlockSpec((1, w), lambda b, s: (0, b)),
    ]
    in_specs = [pl.BlockSpec((tok, w), lambda b, s: (r0 + s, col_block + b))] + mat_specs
    args = [proj, ge, kt, f, apr, api, d_skip.reshape(1, width)]
    if has_h0:
        in_specs += [pl.BlockSpec((rows, ns), lambda b, s: (0, b))] * 2
        args += [h0[0], h0[1]]
        state_shape = jax.ShapeDtypeStruct((rows, nbun * ns), F32)
        state_spec = pl.BlockSpec((rows, ns), lambda b, s: (0, b))
    else:
        state_shape = jax.ShapeDtypeStruct((n_seq, 1, nbun * ns), F32)
        state_spec = pl.BlockSpec((None, 1, ns), lambda b, s: (s, 0, b))
    y, hr, hi = pl.pallas_call(
        functools.partial(_ssm_body, rows=rows, ns=ns, scan_steps=scan_steps, has_h0=has_h0),
        out_shape=(jax.ShapeDtypeStruct((steps * tok, width), F32), state_shape, state_shape),
        grid=(nbun, steps),
        in_specs=in_specs,
        out_specs=(pl.BlockSpec((tok, w), lambda b, s: (s, b)), state_spec, state_spec),
        compiler_params=_params(2),
        name="ssm_scan",
    )(*args)
    return y, hr.reshape(n_seq, nbun * ns), hi.reshape(n_seq, nbun * ns)


def _glu_up_body(te_ref, val_ref, ns_ref, a_ref, wg_ref, wu_ref, o_ref, wg_s, wu_s, *, sub, parts):
    i = pl.program_id(0)

    for part in range(parts):
        rows = slice(part * sub, (part + 1) * sub)
        ok = val_ref[parts * i + part] > 0

        @pl.when(ok)
        def _():
            if part == 0:
                wg_s[...] = wg_ref[...].astype(BF16)
                wu_s[...] = wu_ref[...].astype(BF16)
            a = a_ref[rows, :]
            g = _dot(a, wg_s[...])
            u = _dot(a, wu_s[...])
            o_ref[rows, :] = (g * _sigmoid(g) * u).astype(o_ref.dtype)

        @pl.when(jnp.logical_not(ok))
        def _():
            o_ref[rows, :] = jnp.zeros((sub, o_ref.shape[1]), o_ref.dtype)


def _glu_up(a, w_gate, w_up, lead, tile_expert, sub_valid, n_super, *, sub, parts, bn):
    m, k = a.shape
    n = w_gate.shape[-1]
    bm = parts * sub
    nt = m // bm
    bn = _tile(n, bn)
    nj = n // bn
    lead = tuple(lead)

    def row(i, ns):
        return jnp.minimum(i, ns[0] - 1)

    def col(i, j, ns):
        return jnp.where(i < ns[0], j, nj - 1)

    w_spec = pl.BlockSpec((None,) * (len(lead) + 1) + (k, bn),
                          lambda i, j, te, val, ns: lead + (te[row(i, ns)], 0, col(i, j, ns)))
    return pl.pallas_call(
        functools.partial(_glu_up_body, sub=sub, parts=parts),
        out_shape=jax.ShapeDtypeStruct((m, n), BF16),
        grid_spec=pltpu.PrefetchScalarGridSpec(
            num_scalar_prefetch=3,
            grid=(nt, nj),
            in_specs=[pl.BlockSpec((bm, k), lambda i, j, te, val, ns: (row(i, ns), 0), pipeline_mode=pl.Buffered(1)),
                      w_spec, w_spec],
            out_specs=pl.BlockSpec((bm, bn), lambda i, j, te, val, ns: (i, j)),
            scratch_shapes=[pltpu.VMEM((k, bn), BF16), pltpu.VMEM((k, bn), BF16)],
        ),
        compiler_params=_params(2),
        name="swiglu_up",
    )(tile_expert, sub_valid, n_super, a, w_gate, w_up)


def _glu_down_body(te_ref, val_ref, ns_ref, h_ref, w_ref, o_ref, w_s, *, sub, parts):
    i, k = pl.program_id(0), pl.program_id(2)

    for part in range(parts):
        rows = slice(part * sub, (part + 1) * sub)
        ok = val_ref[parts * i + part] > 0

        @pl.when(jnp.logical_and(jnp.logical_not(ok), k == 0))
        def _():
            o_ref[rows, :] = jnp.zeros((sub, o_ref.shape[1]), o_ref.dtype)

        @pl.when(ok)
        def _():
            if part == 0:
                w_s[...] = w_ref[...].astype(BF16)
            p = _dot(h_ref[rows, :], w_s[...])

            @pl.when(k == 0)
            def _():
                o_ref[rows, :] = p

            @pl.when(k > 0)
            def _():
                o_ref[rows, :] += p


def _glu_down(h, w_down, lead, tile_expert, sub_valid, n_super, *, sub, parts, bk, bn):
    m, kdim = h.shape
    n = w_down.shape[-1]
    bm = parts * sub
    nt = m // bm
    bk = _tile(kdim, bk)
    bn = _tile(n, bn)
    nk, nn = kdim // bk, n // bn
    lead = tuple(lead)

    def row(i, ns):
        return jnp.minimum(i, ns[0] - 1)

    def frozen(i, v, last, ns):
        return jnp.where(i < ns[0], v, last)

    return pl.pallas_call(
        functools.partial(_glu_down_body, sub=sub, parts=parts),
        out_shape=jax.ShapeDtypeStruct((m, n), F32),
        grid_spec=pltpu.PrefetchScalarGridSpec(
            num_scalar_prefetch=3,
            grid=(nt, nn, nk),
            in_specs=[
                pl.BlockSpec((bm, bk), lambda i, j, k, te, val, ns: (row(i, ns), frozen(i, k, nk - 1, ns))),
                pl.BlockSpec((None,) * (len(lead) + 1) + (bk, bn),
                             lambda i, j, k, te, val, ns: lead + (te[row(i, ns)], frozen(i, k, nk - 1, ns),
                                                                  frozen(i, j, nn - 1, ns))),
            ],
            out_specs=pl.BlockSpec((bm, bn), lambda i, j, k, te, val, ns: (i, j)),
            scratch_shapes=[pltpu.VMEM((bk, bn), BF16)],
        ),
        compiler_params=_params(3),
        name="swiglu_down",
    )(tile_expert, sub_valid, n_super, h, w_down)


def _router_body(x_ref, g_ref, wr_ref, route_ref, *, n_experts):
    x = x_ref[...]
    xn = x * lax.rsqrt(jnp.mean(x * x, axis=-1, keepdims=True) + RMS_EPS) * g_ref[...]
    logits = _dot3(xn, wr_ref[...])
    lane = lax.broadcasted_iota(jnp.int32, logits.shape, 1)
    valid = lane < n_experts
    lg = jnp.where(valid, logits, -jnp.inf)
    ex = jnp.where(valid, jnp.exp(lg - jnp.max(lg, axis=-1, keepdims=True)), 0.0)
    probs = ex / jnp.sum(ex, axis=-1, keepdims=True)
    big = logits.shape[-1]
    p1 = jnp.max(probs, axis=-1, keepdims=True)
    i1 = jnp.min(jnp.where(probs == p1, lane, big), axis=-1, keepdims=True)
    rest = jnp.where(lane == i1, -1.0, jnp.where(valid, probs, -1.0))
    p2 = jnp.max(rest, axis=-1, keepdims=True)
    i2 = jnp.min(jnp.where(rest == p2, lane, big), axis=-1, keepdims=True)
    s = p1 + p2
    out = jnp.where(lane == 0, i1.astype(F32), 0.0)
    out = jnp.where(lane == 1, i2.astype(F32), out)
    out = jnp.where(lane == 2, p1 / s, out)
    out = jnp.where(lane == 3, p2 / s, out)
    route_ref[...] = out


def _router(x, g, w_router):
    t, d = x.shape
    n_experts = w_router.shape[-1]
    wr = jnp.pad(w_router, ((0, 0), (0, LANES - n_experts)))
    bm = _tile(t, 256, 8)
    return pl.pallas_call(
        functools.partial(_router_body, n_experts=n_experts),
        out_shape=jax.ShapeDtypeStruct((t, LANES), F32),
        grid=(t // bm,),
        in_specs=[pl.BlockSpec((bm, d), lambda i: (i, 0)), pl.BlockSpec((1, d), lambda i: (0, 0)),
                  pl.BlockSpec((d, LANES), lambda i: (0, 0))],
        out_specs=pl.BlockSpec((bm, LANES), lambda i: (i, 0)),
        compiler_params=_params(1),
        name="moe_router",
    )(x, g.reshape(1, d), wr)


def _row_copy(src_hbm, src_row, dst_ref, dst_row, sem):
    return pltpu.make_async_copy(src_hbm.at[pl.ds(src_row, 1), :], dst_ref.at[pl.ds(dst_row, 1), :], sem)


def _gather_norm_body(val_ref, idx_ref, x_hbm, g_ref, o_ref, buf, sem, *, bm):
    ok = val_ref[pl.program_id(0)] > 0

    @pl.when(ok)
    def _():
        def start(r, c):
            _row_copy(x_hbm, idx_ref[0, 0, r], buf, r, sem).start()
            return c

        def wait(r, c):
            _row_copy(x_hbm, 0, buf, r, sem).wait()
            return c

        lax.fori_loop(0, bm, start, 0)
        lax.fori_loop(0, bm, wait, 0)
        x = buf[...]
        inv = lax.rsqrt(jnp.mean(x * x, axis=-1, keepdims=True) + RMS_EPS)
        o_ref[...] = (x * inv * g_ref[...]).astype(o_ref.dtype)

    @pl.when(jnp.logical_not(ok))
    def _():
        o_ref[...] = jnp.zeros(o_ref.shape, o_ref.dtype)


def _gather_norm(x, g, idx, sub_valid, *, bm):
    n = idx.shape[0]
    d = x.shape[1]
    nt = n // bm
    return pl.pallas_call(
        functools.partial(_gather_norm_body, bm=bm),
        out_shape=jax.ShapeDtypeStruct((n, d), BF16),
        grid_spec=pltpu.PrefetchScalarGridSpec(
            num_scalar_prefetch=1,
            grid=(nt,),
            in_specs=[pl.BlockSpec((1, 1, bm), lambda i, val: (i, 0, 0), memory_space=pltpu.SMEM),
                      pl.BlockSpec(memory_space=pl.ANY),
                      pl.BlockSpec((1, d), lambda i, val: (0, 0))],
            out_specs=pl.BlockSpec((bm, d), lambda i, val: (i, 0)),
            scratch_shapes=[pltpu.VMEM((bm, d), F32), pltpu.SemaphoreType.DMA],
        ),
        compiler_params=_params(1),
        name="moe_gather",
    )(sub_valid, idx.reshape(nt, 1, bm), x, g.reshape(1, d))


def _combine_body(d1_ref, d2_ref, x_ref, route_ref, g_ref, ys_hbm, o_ref, buf1, buf2, sem, *, bm, final_norm):
    def start(r, c):
        _row_copy(ys_hbm, d1_ref[0, 0, r], buf1, r, sem).start()
        _row_copy(ys_hbm, d2_ref[0, 0, r], buf2, r, sem).start()
        return c

    def wait(r, c):
        _row_copy(ys_hbm, 0, buf1, r, sem).wait()
        _row_copy(ys_hbm, 0, buf2, r, sem).wait()
        return c

    lax.fori_loop(0, bm, start, 0)
    lax.fori_loop(0, bm, wait, 0)
    route = route_ref[...]
    x = x_ref[...] + route[:, 2:3] * buf1[...] + route[:, 3:4] * buf2[...]
    if final_norm:
        x = x * lax.rsqrt(jnp.mean(x * x, axis=-1, keepdims=True) + RMS_EPS) * g_ref[...]
    o_ref[...] = x


def _combine(x, route, ys, d1, d2, g, *, final_norm, bm=128):
    t, d = x.shape
    bm = _tile(t, bm, 8)
    nt = t // bm
    idx_spec = pl.BlockSpec((1, 1, bm), lambda i: (i, 0, 0), memory_space=pltpu.SMEM)
    return pl.pallas_call(
        functools.partial(_combine_body, bm=bm, final_norm=final_norm),
        out_shape=jax.ShapeDtypeStruct((t, d), F32),
        grid=(nt,),
        in_specs=[idx_spec, idx_spec,
                  pl.BlockSpec((bm, d), lambda i: (i, 0)),
                  pl.BlockSpec((bm, LANES), lambda i: (i, 0)),
                  pl.BlockSpec((1, d), lambda i: (0, 0)),
                  pl.BlockSpec(memory_space=pl.ANY)],
        out_specs=pl.BlockSpec((bm, d), lambda i: (i, 0)),
        scratch_shapes=[pltpu.VMEM((bm, d), F32), pltpu.VMEM((bm, d), F32), pltpu.SemaphoreType.DMA],
        compiler_params=_params(1),
        name="moe_combine",
    )(d1.reshape(nt, 1, bm), d2.reshape(nt, 1, bm), x, route, g.reshape(1, d), ys)


def _moe(x, g, w_router, w_gate, w_up, w_down, lead, final_gain, *, sub=256, parts=4):
    t, d = x.shape
    n_experts = w_router.shape[-1]
    route = _router(x, g, w_router)
    sup = parts * sub
    eid = route[:, :TOP_K].astype(jnp.int32).reshape(-1)
    onehot = (eid[:, None] == jnp.arange(n_experts)[None, :]).astype(jnp.int32)
    rank = jnp.sum((jnp.cumsum(onehot, axis=0) - onehot) * onehot, axis=1)
    count = jnp.sum(onehot, axis=0)
    sup_per = (count + sup - 1) // sup
    sup_end = jnp.cumsum(sup_per)
    sup_start = sup_end - sup_per
    dest = sup_start[eid] * sup + rank
    n_tiles = (TOP_K * t) // sup + n_experts
    n_super = sup_end[-1:].astype(jnp.int32)
    tile_expert = jnp.minimum(jnp.sum(jnp.arange(n_tiles)[:, None] >= sup_end[None, :], axis=1),
                              n_experts - 1).astype(jnp.int32)
    s = jnp.arange(parts * n_tiles)
    e_of = tile_expert[s // parts]
    sub_valid = (((s - parts * sup_start[e_of]) * sub < count[e_of]) & (s // parts < n_super[0])).astype(jnp.int32)
    token = jnp.arange(TOP_K * t, dtype=jnp.int32) // TOP_K
    src = jnp.zeros((n_tiles * sup,), jnp.int32).at[dest].set(token)
    xs = _gather_norm(x, g, src, sub_valid, bm=sub)
    h = _glu_up(xs, w_gate, w_up, lead, tile_expert, sub_valid, n_super, sub=sub, parts=parts, bn=512)
    ys = _glu_down(h, w_down, lead, tile_expert, sub_valid, n_super, sub=sub, parts=parts, bk=2048, bn=1024)
    dest2 = dest.reshape(t, TOP_K).astype(jnp.int32)
    gain = final_gain if final_gain is not None else jnp.ones((d,), F32)
    return _combine(x, route, ys, dest2[:, 0], dest2[:, 1], gain, final_norm=final_gain is not None)


def kernel(x_prompt, x_sample, state_ssm_re, state_ssm_im, norm_mix, norm_ffn, norm_final,
           w_in, gmlp_ln_g, gmlp_ln_b, gmlp_w_s, gmlp_b_s, w_proj_a,
           ssm_a_re, ssm_a_im, ssm_log_dt, ssm_b_re, ssm_b_im, ssm_c_re, ssm_c_im, ssm_d,
           w_glu, b_glu, w_proj_b, w_out,
           ffn_w_gate, ffn_w_up, ffn_w_down,
           moe_router, moe_w_gate, moe_w_up, moe_w_down):
    nb_, seq, d = x_prompt.shape
    db, ds, _ = x_sample.shape
    tp, ts = nb_ * seq, db * ds
    t = tp + ts
    depth = w_in.shape[0]
    gw = gmlp_ln_g.shape[1]
    sw = ssm_d.shape[1]
    groups, state = ssm_a_re.shape[1:]
    assert ds == SSM_BLOCK and gw % LANES == 0 and sw % LANES == 0 and d % LANES == 0
    xb_col = (2 * gw) // LANES
    x = jnp.concatenate([x_prompt.reshape(tp, d), x_sample.reshape(ts, d)], axis=0)

    hre_p, him_p, hre_s, him_s, v_rows = [], [], [], [], []
    for l in range(depth):
        xn = _rmsnorm(x, norm_mix[l], BF16)
        proj = _matmul(xn, w_in, (l,), F32)
        gated, v_all = _gmlp(proj, gmlp_ln_g[l], gmlp_ln_b[l], gmlp_w_s[l], gmlp_b_s[l], width=gw, tp=tp, dec_seq=ds)
        v_rows.append(v_all[tp:].reshape(db, ds, gw))
        branch_a = _matmul(gated, w_proj_a, (l,), F32)
        mats = _ssm_prep(ssm_a_re[l], ssm_a_im[l], ssm_log_dt[l], ssm_b_re[l], ssm_b_im[l], ssm_c_re[l], ssm_c_im[l])
        y_p, hr, hi = _ssm_scan(proj, mats, ssm_d[l], col_block=xb_col, row_start=0, n_seq=nb_, seq_len=seq)
        hre_p.append(hr.reshape(nb_, groups, state))
        him_p.append(hi.reshape(nb_, groups, state))
        h0 = (state_ssm_re[l].reshape(db, groups * state), state_ssm_im[l].reshape(db, groups * state))
        y_s, hr, hi = _ssm_scan(proj, mats, ssm_d[l], col_block=xb_col, row_start=tp, n_seq=db, seq_len=ds, h0=h0)
        hre_s.append(hr.reshape(db, groups, state))
        him_s.append(hi.reshape(db, groups, state))
        yg = jnp.concatenate([y_p, y_s], axis=0)
        bn_glu = _tile(sw, 512)
        y_glu = _matmul(yg, w_glu, (l,), BF16, bn=bn_glu, epilogue=_ep_glu_gate,
                        extras=[(yg, "tile", 0), (b_glu[l].reshape(1, sw), "row", 0)])
        bn_m = math.gcd(_tile(d, 512), 2 * gw + sw, 2 * gw + sw + d)
        merged = _matmul(y_glu, w_proj_b, (l,), BF16, bn=bn_m, epilogue=_ep_merge,
                         extras=[(branch_a, "tile", 0), (proj, "tile", (2 * gw + sw) // bn_m),
                                 (proj, "tile", (2 * gw + sw + d) // bn_m)])
        x = _matmul(merged, w_out, (l,), F32, epilogue=_ep_residual, extras=[(x, "tile", 0)])
        j = l // 2
        last = l == depth - 1
        if l % 2 == 0:
            xn = _rmsnorm(x, norm_ffn[l], BF16)
            sub = _tile(t, 1024, 8)
            nt = t // sub
            h = _glu_up(xn, ffn_w_gate, ffn_w_up, (), jnp.full((nt,), j, jnp.int32), jnp.ones((nt,), jnp.int32),
                        jnp.full((1,), nt, jnp.int32), sub=sub, parts=1, bn=256)
            x = _matmul(h, ffn_w_down, (j,), F32, bm=1024, bn=256, epilogue=_ep_residual, extras=[(x, "tile", 0)],
                        a_buffers=1)
            if last:
                x = _rmsnorm(x, norm_final, F32)
        else:
            x = _moe(x, norm_ffn[l], moe_router[j], moe_w_gate, moe_w_up, moe_w_down, (j,),
                     norm_final if last else None)
    y = x
    return (y[:tp].reshape(nb_, seq, d), y[tp:].reshape(db, ds, d),
            jnp.stack(hre_p), jnp.stack(him_p), jnp.stack(hre_s), jnp.stack(him_s), jnp.stack(v_rows))
```

```python
import functools
import math

import jax
import jax.numpy as jnp
from jax import lax
from jax.experimental import pallas as pl
from jax.experimental.pallas import tpu as pltpu

F32 = jnp.float32
BF16 = jnp.bfloat16

RMS_EPS = 1e-5
LN_EPS = 1e-5
TOP_K = 2

LANES = 128
SSM_BLOCK = 8
GROUPS_PER_BUNDLE = 8
V7X_VMEM_LIMIT = 56 * 1024 * 1024


def _params(n_axes, vmem=V7X_VMEM_LIMIT):
    return pltpu.CompilerParams(dimension_semantics=("arbitrary",) * n_axes, vmem_limit_bytes=vmem)


def _tile(n, pref, mult=LANES):
    if n <= pref:
        return n
    t = (pref // mult) * mult
    while t >= mult:
        if n % t == 0:
            return t
        t -= mult
    raise ValueError(f"no tile for {n} <= {pref}")


def _gelu(x):
    return 0.5 * x * (1.0 + jnp.tanh(0.7978845608028654 * (x + 0.044715 * (x * x * x))))


def _sigmoid(x):
    return 1.0 / (1.0 + jnp.exp(-x))


def _dot(a, b):
    return jnp.dot(a, b, preferred_element_type=F32)


def _dot3(a, b):
    ah = a.astype(BF16)
    al = (a - ah.astype(F32)).astype(BF16)
    bh = b.astype(BF16)
    bl = (b - bh.astype(F32)).astype(BF16)
    return _dot(ah, bh) + _dot(ah, bl) + _dot(al, bh)


def _rmsnorm_body(x_ref, g_ref, o_ref):
    x = x_ref[...]
    inv = lax.rsqrt(jnp.mean(x * x, axis=-1, keepdims=True) + RMS_EPS)
    o_ref[...] = (x * inv * g_ref[...]).astype(o_ref.dtype)


def _rmsnorm(x, g, out_dtype):
    t, d = x.shape
    bm = _tile(t, 256, 8)
    return pl.pallas_call(
        _rmsnorm_body,
        out_shape=jax.ShapeDtypeStruct((t, d), out_dtype),
        grid=(t // bm,),
        in_specs=[pl.BlockSpec((bm, d), lambda i: (i, 0)), pl.BlockSpec((1, d), lambda i: (0, 0))],
        out_specs=pl.BlockSpec((bm, d), lambda i: (i, 0)),
        compiler_params=_params(1),
        name="rmsnorm",
    )(x, g.reshape(1, d))


def _mm_body(*refs, n_extra, epilogue):
    a_ref, w_ref = refs[0], refs[1]
    extras = refs[2:2 + n_extra]
    o_ref = refs[2 + n_extra]
    acc = _dot(a_ref[...].astype(BF16), w_ref[...].astype(BF16))
    if epilogue is not None:
        acc = epilogue(acc, *[r[...] for r in extras])
    o_ref[...] = acc.astype(o_ref.dtype)


def _matmul(a, w, lead, out_dtype, *, bm=1024, bn=512, epilogue=None, extras=(), a_buffers=2):
    m, k = a.shape
    n = w.shape[-1]
    assert w.shape[-2] == k
    bm = _tile(m, bm, 8)
    bn = _tile(n, bn)
    lead = tuple(lead)
    a_kwargs = {} if a_buffers == 2 else {"pipeline_mode": pl.Buffered(a_buffers)}
    in_specs = [
        pl.BlockSpec((bm, k), lambda i, j: (i, 0), **a_kwargs),
        pl.BlockSpec((None,) * len(lead) + (k, bn), lambda i, j: lead + (0, j)),
    ]
    args = [a, w]
    for arr, kind, off in extras:
        if kind == "tile":
            in_specs.append(pl.BlockSpec((bm, bn), lambda i, j, off=off: (i, j + off)))
        else:
            in_specs.append(pl.BlockSpec((1, bn), lambda i, j: (0, j)))
        args.append(arr)
    return pl.pallas_call(
        functools.partial(_mm_body, n_extra=len(extras), epilogue=epilogue),
        out_shape=jax.ShapeDtypeStruct((m, n), out_dtype),
        grid=(m // bm, n // bn),
        in_specs=in_specs,
        out_specs=pl.BlockSpec((bm, bn), lambda i, j: (i, j)),
        compiler_params=_params(2),
        name="matmul",
    )(*args)


def _ep_residual(acc, res):
    return res + acc


def _ep_glu_gate(acc, y, b):
    return y * _sigmoid(acc + b)


def _ep_merge(acc, branch_a, g_a, g_b):
    return _sigmoid(g_a) * branch_a + _sigmoid(g_b) * acc


def _gmlp_body(u_ref, v_ref, g_ref, b_ref, w_ref, bias_ref, o_ref, vout_ref, *, heads, hd):
    v = _gelu(v_ref[...])
    mu = jnp.mean(v, axis=-1, keepdims=True)
    vc = v - mu
    var = jnp.mean(vc * vc, axis=-1, keepdims=True)
    vn = vc * lax.rsqrt(var + LN_EPS) * g_ref[...] + b_ref[...]
    vout_ref[...] = vn
    c = vn.shape[0]
    causal = lax.broadcasted_iota(jnp.int32, (c, c), 1) <= lax.broadcasted_iota(jnp.int32, (c, c), 0)
    for h in range(heads):
        sl = slice(h * hd, (h + 1) * hd)
        w = jnp.where(causal, w_ref[0, h], 0.0).astype(BF16)
        mixed = _dot(w, vn[:, sl].astype(BF16)) + bias_ref[0, :, h:h + 1]
        o_ref[:, sl] = (_gelu(u_ref[:, sl]) * mixed).astype(o_ref.dtype)


def _gmlp(proj, ln_g, ln_b, w_s, b_s, *, width, tp, dec_seq):
    t = proj.shape[0]
    heads, chunk, _ = w_s.shape
    hd = width // heads
    assert tp % chunk == 0 and (t - tp) % chunk == 0 and chunk % dec_seq == 0
    rep = chunk // dec_seq
    blk = jnp.arange(chunk) // dec_seq
    same = (blk[:, None] == blk[None, :]).astype(F32)
    w_sample = jnp.tile(w_s[:, :dec_seq, :dec_seq], (1, rep, rep)) * same[None]
    w_eff = jnp.stack([w_s, w_sample])
    bias_eff = jnp.stack([b_s.T, jnp.tile(b_s[:, :dec_seq].T, (rep, 1))])
    n_prompt_chunks = tp // chunk
    variant = lambda c: jnp.where(c >= n_prompt_chunks, 1, 0)
    return pl.pallas_call(
        functools.partial(_gmlp_body, heads=heads, hd=hd),
        out_shape=(jax.ShapeDtypeStruct((t, width), BF16), jax.ShapeDtypeStruct((t, width), F32)),
        grid=(t // chunk,),
        in_specs=[
            pl.BlockSpec((chunk, width), lambda c: (c, 0)),
            pl.BlockSpec((chunk, width), lambda c: (c, 1)),
            pl.BlockSpec((1, width), lambda c: (0, 0)),
            pl.BlockSpec((1, width), lambda c: (0, 0)),
            pl.BlockSpec((1, heads, chunk, chunk), lambda c: (variant(c), 0, 0, 0)),
            pl.BlockSpec((1, chunk, heads), lambda c: (variant(c), 0, 0)),
        ],
        out_specs=(pl.BlockSpec((chunk, width), lambda c: (c, 0)), pl.BlockSpec((chunk, width), lambda c: (c, 0))),
        compiler_params=_params(1),
        name="gmlp_spatial",
    )(proj, proj, ln_g.reshape(1, width), ln_b.reshape(1, width), w_eff, bias_eff)


def _ssm_prep_body(lr_ref, li_ref, ldt_ref, lrc_ref, lic_ref, ldtc_ref, btr_ref, bti_ref, cmr_ref, cmi_ref,
                   ge_ref, kt_ref, f_ref, apr_ref, api_ref, *, ns, ch, state):
    nb = SSM_BLOCK
    w = GROUPS_PER_BUNDLE * ch

    def powers(lr, li, ldt):
        dt = jnp.exp(ldt)
        e = jnp.exp(lr * dt)
        ar = e * jnp.cos(li * dt)
        ai = e * jnp.sin(li * dt)
        ps = [(jnp.ones_like(ar), jnp.zeros_like(ar))]
        for _ in range(nb):
            pr, pi = ps[-1]
            ps.append((pr * ar - pi * ai, pr * ai + pi * ar))
        return ar, ai, ps

    lr, li = lr_ref[0], li_ref[0]
    ar, ai, prow = powers(lr, li, ldt_ref[0])
    _, _, pcol = powers(lrc_ref[0], lic_ref[0], ldtc_ref[0])
    den = lr * lr + li * li
    rr = ((ar - 1.0) * lr + ai * li) / den
    ri = (ai * lr - (ar - 1.0) * li) / den

    mask_b = (lax.broadcasted_iota(jnp.int32, (w, ns), 0) // ch) == (lax.broadcasted_iota(jnp.int32, (w, ns), 1) // state)
    mask_c = (lax.broadcasted_iota(jnp.int32, (ns, w), 0) // state) == (lax.broadcasted_iota(jnp.int32, (ns, w), 1) // ch)
    btr, bti = btr_ref[0], bti_ref[0]
    bbr = jnp.where(mask_b, rr * btr - ri * bti, 0.0)
    bbi = jnp.where(mask_b, rr * bti + ri * btr, 0.0)
    cmr = jnp.where(mask_c, cmr_ref[0], 0.0)
    cmi = jnp.where(mask_c, cmi_ref[0], 0.0)

    kt_ref[...] = jnp.zeros(kt_ref.shape, kt_ref.dtype)
    for d in range(nb):
        pr, pi = prow[d]
        bdr = bbr * pr - bbi * pi
        bdi = bbr * pi + bbi * pr
        i = nb - 1 - d
        ge_ref[0, i * w:(i + 1) * w, 0:ns] = bdr.astype(ge_ref.dtype)
        ge_ref[0, i * w:(i + 1) * w, ns:2 * ns] = bdi.astype(ge_ref.dtype)
        kd = (_dot3(bdr, cmr) - _dot3(bdi, cmi)).astype(kt_ref.dtype)
        for i0 in range(nb - d):
            kt_ref[0, i0 * w:(i0 + 1) * w, (i0 + d) * w:(i0 + d + 1) * w] = kd
    for j in range(nb):
        pr, pi = pcol[j + 1]
        f_ref[0, 0:ns, j * w:(j + 1) * w] = (cmr * pr - cmi * pi).astype(f_ref.dtype)
        f_ref[0, ns:2 * ns, j * w:(j + 1) * w] = (-(cmr * pi + cmi * pr)).astype(f_ref.dtype)
    qr, qi = prow[nb]
    for k in range(apr_ref.shape[1]):
        apr_ref[0, k:k + 1, :] = qr
        api_ref[0, k:k + 1, :] = qi
        qr, qi = qr * qr - qi * qi, 2.0 * qr * qi


def _ssm_prep(a_re, a_im, log_dt, b_re, b_im, c_re, c_im):
    g, state = a_re.shape
    ch = b_re.shape[-1]
    gb = GROUPS_PER_BUNDLE
    assert g % gb == 0 and gb * ch == LANES
    nbun = g // gb
    ns = gb * state
    w = gb * ch
    nb = SSM_BLOCK
    ldt = jnp.repeat(log_dt, state)
    rows = [v.reshape(nbun, 1, ns) for v in (a_re, a_im, ldt)]
    cols = [v.reshape(nbun, ns, 1) for v in (a_re, a_im, ldt)]

    def b_layout(b):
        return jnp.tile(b.transpose(0, 2, 1).reshape(nbun, w, state), (1, 1, gb))

    def c_layout(c):
        cm = c.transpose(0, 2, 1).reshape(nbun, gb, state, ch).transpose(0, 2, 1, 3).reshape(nbun, state, w)
        return jnp.tile(cm, (1, gb, 1))

    spec_row = pl.BlockSpec((1, 1, ns), lambda b: (b, 0, 0))
    spec_col = pl.BlockSpec((1, ns, 1), lambda b: (b, 0, 0))
    spec_b = pl.BlockSpec((1, w, ns), lambda b: (b, 0, 0))
    spec_c = pl.BlockSpec((1, ns, w), lambda b: (b, 0, 0))
    return pl.pallas_call(
        functools.partial(_ssm_prep_body, ns=ns, ch=ch, state=state),
        out_shape=(
            jax.ShapeDtypeStruct((nbun, nb * w, 2 * ns), BF16),
            jax.ShapeDtypeStruct((nbun, nb * w, nb * w), BF16),
            jax.ShapeDtypeStruct((nbun, 2 * ns, nb * w), BF16),
            jax.ShapeDtypeStruct((nbun, 8, ns), F32),
            jax.ShapeDtypeStruct((nbun, 8, ns), F32),
        ),
        grid=(nbun,),
        in_specs=[spec_row] * 3 + [spec_col] * 3 + [spec_b] * 2 + [spec_c] * 2,
        out_specs=(
            pl.BlockSpec((1, nb * w, 2 * ns), lambda b: (b, 0, 0)),
            pl.BlockSpec((1, nb * w, nb * w), lambda b: (b, 0, 0)),
            pl.BlockSpec((1, 2 * ns, nb * w), lambda b: (b, 0, 0)),
            pl.BlockSpec((1, 8, ns), lambda b: (b, 0, 0)),
            pl.BlockSpec((1, 8, ns), lambda b: (b, 0, 0)),
        ),
        compiler_params=_params(1),
        name="ssm_prep",
    )(*rows, *cols, b_layout(b_re), b_layout(b_im), c_layout(c_re), c_layout(c_im))


def _ssm_body(*refs, rows, ns, scan_steps, has_h0):
    if has_h0:
        x_ref, ge_ref, kt_ref, f_ref, apr_ref, api_ref, d_ref, h0r_ref, h0i_ref, y_ref, hr_ref, hi_ref = refs
    else:
        x_ref, ge_ref, kt_ref, f_ref, apr_ref, api_ref, d_ref, y_ref, hr_ref, hi_ref = refs
    nb = SSM_BLOCK
    us = [x_ref[pl.ds(j, rows, stride=nb), :] for j in range(nb)]
    u = jnp.concatenate(us, axis=1).astype(BF16)
    e = _dot(u, ge_ref[0])
    cr, ci = e[:, :ns], e[:, ns:]
    if has_h0:
        pr, pi = h0r_ref[...], h0i_ref[...]
        ar, ai = apr_ref[0, 0:1, :], api_ref[0, 0:1, :]
        cr = ar * pr - ai * pi + cr
        ci = ar * pi + ai * pr + ci
        hr_ref[...] = cr
        hi_ref[...] = ci
    else:
        row = lax.broadcasted_iota(jnp.int32, (rows, 1), 0)
        for k in range(scan_steps):
            sh = 1 << k
            keep = row >= sh
            sr = jnp.where(keep, pltpu.roll(cr, sh, 0), 0.0)
            si = jnp.where(keep, pltpu.roll(ci, sh, 0), 0.0)
            ar, ai = apr_ref[0, k:k + 1, :], api_ref[0, k:k + 1, :]
            cr, ci = cr + ar * sr - ai * si, ci + ar * si + ai * sr
        keep = row >= 1
        pr = jnp.where(keep, pltpu.roll(cr, 1, 0), 0.0)
        pi = jnp.where(keep, pltpu.roll(ci, 1, 0), 0.0)
        hr_ref[...] = cr[rows - 1:rows, :]
        hi_ref[...] = ci[rows - 1:rows, :]
    carry = jnp.concatenate([pr, pi], axis=1).astype(BF16)
    y = _dot(u, kt_ref[0]) + _dot(carry, f_ref[0])
    d = d_ref[...]
    w = d.shape[-1]
    for j in range(nb):
        y_ref[pl.ds(j, rows, stride=nb), :] = _gelu(y[:, j * w:(j + 1) * w] + d * us[j])


def _ssm_scan(proj, mats, d_skip, *, col_block, row_start, n_seq, seq_len, h0=None):
    ge, kt, f, apr, api = mats
    nbun, _, ns2 = ge.shape
    ns = ns2 // 2
    w = LANES
    nb = SSM_BLOCK
    width = nbun * w
    has_h0 = h0 is not None
    if has_h0:
        assert seq_len == nb
        rows = n_seq
        steps, tok = 1, n_seq * nb
        scan_steps = 0
    else:
        rows = seq_len // nb
        steps, tok = n_seq, seq_len
        scan_steps = int(math.log2(rows))
        assert 1 << scan_steps == rows and scan_steps <= apr.shape[1]
    assert row_start % tok == 0
    r0 = row_start // tok
    mat_specs = [
        pl.BlockSpec((1,) + ge.shape[1:], lambda b, s: (b, 0, 0)),
        pl.BlockSpec((1,) + kt.shape[1:], lambda b, s: (b, 0, 0)),
        pl.BlockSpec((1,) + f.shape[1:], lambda b, s: (b, 0, 0)),
        pl.BlockSpec((1,) + apr.shape[1:], lambda b, s: (b, 0, 0)),
        pl.BlockSpec((1,) + api.shape[1:], lambda b, s: (b, 0, 0)),
        pl.BlockSpec((1, w), lambda b, s: (0, b)),
    ]
    in_specs = [pl.BlockSpec((tok, w), lambda b, s: (r0 + s, col_block + b))] + mat_specs
    args = [proj, ge, kt, f, apr, api, d_skip.reshape(1, width)]
    if has_h0:
        in_specs += [pl.BlockSpec((rows, ns), lambda b, s: (0, b))] * 2
        args += [h0[0], h0[1]]
        state_shape = jax.ShapeDtypeStruct((rows, nbun * ns), F32)
        state_spec = pl.BlockSpec((rows, ns), lambda b, s: (0, b))
    else:
        state_shape = jax.ShapeDtypeStruct((n_seq, 1, nbun * ns), F32)
        state_spec = pl.BlockSpec((None, 1, ns), lambda b, s: (s, 0, b))
    y, hr, hi = pl.pallas_call(
        functools.partial(_ssm_body, rows=rows, ns=ns, scan_steps=scan_steps, has_h0=has_h0),
        out_shape=(jax.ShapeDtypeStruct((steps * tok, width), F32), state_shape, state_shape),
        grid=(nbun, steps),
        in_specs=in_specs,
        out_specs=(pl.BlockSpec((tok, w), lambda b, s: (s, b)), state_spec, state_spec),
        compiler_params=_params(2),
        name="ssm_scan",
    )(*args)
    return y, hr.reshape(n_seq, nbun * ns), hi.reshape(n_seq, nbun * ns)


def _glu_up_body(te_ref, val_ref, ns_ref, a_ref, wg_ref, wu_ref, o_ref, wg_s, wu_s, *, sub, parts):
    i = pl.program_id(0)

    def cast_weights():
        wg_s[...] = wg_ref[...].astype(BF16)
        wu_s[...] = wu_ref[...].astype(BF16)

    def compute(rows):
        a = a_ref[rows, :]
        g = _dot(a, wg_s[...])
        u = _dot(a, wu_s[...])
        o_ref[rows, :] = (g * _sigmoid(g) * u).astype(o_ref.dtype)

    full = val_ref[parts * i + parts - 1] > 0

    @pl.when(full)
    def _():
        cast_weights()
        bm = parts * sub
        chunk = min(bm, 512)
        for r0 in range(0, bm, chunk):
            compute(slice(r0, r0 + chunk))

    @pl.when(jnp.logical_not(full))
    def _():
        for part in range(parts):
            rows = slice(part * sub, (part + 1) * sub)
            ok = val_ref[parts * i + part] > 0

            if part < parts - 1:
                @pl.when(ok)
                def _():
                    if part == 0:
                        cast_weights()
                    compute(rows)

            @pl.when(jnp.logical_not(ok))
            def _():
                o_ref[rows, :] = jnp.zeros((sub, o_ref.shape[1]), o_ref.dtype)


def _glu_up(a, w_gate, w_up, lead, tile_expert, sub_valid, n_super, *, sub, parts, bn):
    m, k = a.shape
    n = w_gate.shape[-1]
    bm = parts * sub
    nt = m // bm
    bn = _tile(n, bn)
    nj = n // bn
    lead = tuple(lead)

    def row(i, ns):
        return jnp.minimum(i, ns[0] - 1)

    def col(i, j, ns):
        return jnp.where(i < ns[0], j, nj - 1)

    w_spec = pl.BlockSpec((None,) * (len(lead) + 1) + (k, bn),
                          lambda i, j, te, val, ns: lead + (te[row(i, ns)], 0, col(i, j, ns)))
    return pl.pallas_call(
        functools.partial(_glu_up_body, sub=sub, parts=parts),
        out_shape=jax.ShapeDtypeStruct((m, n), BF16),
        grid_spec=pltpu.PrefetchScalarGridSpec(
            num_scalar_prefetch=3,
            grid=(nt, nj),
            in_specs=[pl.BlockSpec((bm, k), lambda i, j, te, val, ns: (row(i, ns), 0), pipeline_mode=pl.Buffered(1)),
                      w_spec, w_spec],
            out_specs=pl.BlockSpec((bm, bn), lambda i, j, te, val, ns: (i, j)),
            scratch_shapes=[pltpu.VMEM((k, bn), BF16), pltpu.VMEM((k, bn), BF16)],
        ),
        compiler_params=_params(2),
        name="swiglu_up",
    )(tile_expert, sub_valid, n_super, a, w_gate, w_up)


def _glu_down_body(te_ref, val_ref, ns_ref, h_ref, w_ref, o_ref, w_s, *, sub, parts):
    i, k = pl.program_id(0), pl.program_id(2)

    def cast_weights():
        w_s[...] = w_ref[...].astype(BF16)

    def accumulate(rows):
        p = _dot(h_ref[rows, :], w_s[...])

        @pl.when(k == 0)
        def _():
            o_ref[rows, :] = p

        @pl.when(k > 0)
        def _():
            o_ref[rows, :] += p

    full = val_ref[parts * i + parts - 1] > 0

    @pl.when(full)
    def _():
        cast_weights()
        accumulate(slice(0, parts * sub))

    @pl.when(jnp.logical_not(full))
    def _():
        for part in range(parts):
            rows = slice(part * sub, (part + 1) * sub)
            ok = val_ref[parts * i + part] > 0

            if part < parts - 1:
                @pl.when(ok)
                def _():
                    if part == 0:
                        cast_weights()
                    accumulate(rows)

            @pl.when(jnp.logical_and(jnp.logical_not(ok), k == 0))
            def _():
                o_ref[rows, :] = jnp.zeros((sub, o_ref.shape[1]), o_ref.dtype)


def _glu_down(h, w_down, lead, tile_expert, sub_valid, n_super, *, sub, parts, bk, bn):
    m, kdim = h.shape
    n = w_down.shape[-1]
    bm = parts * sub
    nt = m // bm
    bk = _tile(kdim, bk)
    bn = _tile(n, bn)
    nk, nn = kdim // bk, n // bn
    lead = tuple(lead)

    def row(i, ns):
        return jnp.minimum(i, ns[0] - 1)

    def frozen(i, v, last, ns):
        return jnp.where(i < ns[0], v, last)

    return pl.pallas_call(
        functools.partial(_glu_down_body, sub=sub, parts=parts),
        out_shape=jax.ShapeDtypeStruct((m, n), F32),
        grid_spec=pltpu.PrefetchScalarGridSpec(
            num_scalar_prefetch=3,
            grid=(nt, nn, nk),
            in_specs=[
                pl.BlockSpec((bm, bk), lambda i, j, k, te, val, ns: (row(i, ns), frozen(i, k, nk - 1, ns))),
                pl.BlockSpec((None,) * (len(lead) + 1) + (bk, bn),
                             lambda i, j, k, te, val, ns: lead + (te[row(i, ns)], frozen(i, k, nk - 1, ns),
                                                                  frozen(i, j, nn - 1, ns))),
            ],
            out_specs=pl.BlockSpec((bm, bn), lambda i, j, k, te, val, ns: (i, j)),
            scratch_shapes=[pltpu.VMEM((bk, bn), BF16)],
        ),
        compiler_params=_params(3),
        name="swiglu_down",
    )(tile_expert, sub_valid, n_super, h, w_down)


def _router_body(x_ref, g_ref, wr_ref, route_ref, *, n_experts):
    x = x_ref[...]
    xn = x * lax.rsqrt(jnp.mean(x * x, axis=-1, keepdims=True) + RMS_EPS) * g_ref[...]
    logits = _dot3(xn, wr_ref[...])
    lane = lax.broadcasted_iota(jnp.int32, logits.shape, 1)
    valid = lane < n_experts
    lg = jnp.where(valid, logits, -jnp.inf)
    ex = jnp.where(valid, jnp.exp(lg - jnp.max(lg, axis=-1, keepdims=True)), 0.0)
    probs = ex / jnp.sum(ex, axis=-1, keepdims=True)
    big = logits.shape[-1]
    p1 = jnp.max(probs, axis=-1, keepdims=True)
    i1 = jnp.min(jnp.where(probs == p1, lane, big), axis=-1, keepdims=True)
    rest = jnp.where(lane == i1, -1.0, jnp.where(valid, probs, -1.0))
    p2 = jnp.max(rest, axis=-1, keepdims=True)
    i2 = jnp.min(jnp.where(rest == p2, lane, big), axis=-1, keepdims=True)
    s = p1 + p2
    out = jnp.where(lane == 0, i1.astype(F32), 0.0)
    out = jnp.where(lane == 1, i2.astype(F32), out)
    out = jnp.where(lane == 2, p1 / s, out)
    out = jnp.where(lane == 3, p2 / s, out)
    route_ref[...] = out


def _router(x, g, w_router):
    t, d = x.shape
    n_experts = w_router.shape[-1]
    wr = jnp.pad(w_router, ((0, 0), (0, LANES - n_experts)))
    bm = _tile(t, 256, 8)
    return pl.pallas_call(
        functools.partial(_router_body, n_experts=n_experts),
        out_shape=jax.ShapeDtypeStruct((t, LANES), F32),
        grid=(t // bm,),
        in_specs=[pl.BlockSpec((bm, d), lambda i: (i, 0)), pl.BlockSpec((1, d), lambda i: (0, 0)),
                  pl.BlockSpec((d, LANES), lambda i: (0, 0))],
        out_specs=pl.BlockSpec((bm, LANES), lambda i: (i, 0)),
        compiler_params=_params(1),
        name="moe_router",
    )(x, g.reshape(1, d), wr)


def _row_copy(src_hbm, src_row, dst_ref, dst_row, sem):
    return pltpu.make_async_copy(src_hbm.at[pl.ds(src_row, 1), :], dst_ref.at[pl.ds(dst_row, 1), :], sem)


def _gather_norm_body(val_ref, idx_ref, x_hbm, g_ref, o_ref, buf, sem, *, bm):
    ok = val_ref[pl.program_id(0)] > 0

    @pl.when(ok)
    def _():
        def start(r, c):
            _row_copy(x_hbm, idx_ref[0, 0, r], buf, r, sem).start()
            return c

        def wait(r, c):
            _row_copy(x_hbm, 0, buf, r, sem).wait()
            return c

        lax.fori_loop(0, bm, start, 0, unroll=8)
        lax.fori_loop(0, bm, wait, 0, unroll=8)
        x = buf[...]
        inv = lax.rsqrt(jnp.mean(x * x, axis=-1, keepdims=True) + RMS_EPS)
        o_ref[...] = (x * inv * g_ref[...]).astype(o_ref.dtype)

    @pl.when(jnp.logical_not(ok))
    def _():
        o_ref[...] = jnp.zeros(o_ref.shape, o_ref.dtype)


def _gather_norm(x, g, idx, sub_valid, *, bm):
    n = idx.shape[0]
    d = x.shape[1]
    nt = n // bm
    return pl.pallas_call(
        functools.partial(_gather_norm_body, bm=bm),
        out_shape=jax.ShapeDtypeStruct((n, d), BF16),
        grid_spec=pltpu.PrefetchScalarGridSpec(
            num_scalar_prefetch=1,
            grid=(nt,),
            in_specs=[pl.BlockSpec((1, 1, bm), lambda i, val: (i, 0, 0), memory_space=pltpu.SMEM),
                      pl.BlockSpec(memory_space=pl.ANY),
                      pl.BlockSpec((1, d), lambda i, val: (0, 0))],
            out_specs=pl.BlockSpec((bm, d), lambda i, val: (i, 0)),
            scratch_shapes=[pltpu.VMEM((bm, d), F32), pltpu.SemaphoreType.DMA],
        ),
        compiler_params=_params(1),
        name="moe_gather",
    )(sub_valid, idx.reshape(nt, 1, bm), x, g.reshape(1, d))


def _combine_body(d1_ref, d2_ref, x_ref, route_ref, g_ref, ys_hbm, o_ref, buf1, buf2, sem, *, bm, final_norm):
    def start(r, c):
        _row_copy(ys_hbm, d1_ref[0, 0, r], buf1, r, sem).start()
        _row_copy(ys_hbm, d2_ref[0, 0, r], buf2, r, sem).start()
        return c

    def wait(r, c):
        _row_copy(ys_hbm, 0, buf1, r, sem).wait()
        _row_copy(ys_hbm, 0, buf2, r, sem).wait()
        return c

    lax.fori_loop(0, bm, start, 0, unroll=8)
    lax.fori_loop(0, bm, wait, 0, unroll=8)
    route = route_ref[...]
    x = x_ref[...] + route[:, 2:3] * buf1[...] + route[:, 3:4] * buf2[...]
    if final_norm:
        x = x * lax.rsqrt(jnp.mean(x * x, axis=-1, keepdims=True) + RMS_EPS) * g_ref[...]
    o_ref[...] = x


def _combine(x, route, ys, d1, d2, g, *, final_norm, bm=128):
    t, d = x.shape
    bm = _tile(t, bm, 8)
    nt = t // bm
    idx_spec = pl.BlockSpec((1, 1, bm), lambda i: (i, 0, 0), memory_space=pltpu.SMEM)
    return pl.pallas_call(
        functools.partial(_combine_body, bm=bm, final_norm=final_norm),
        out_shape=jax.ShapeDtypeStruct((t, d), F32),
        grid=(nt,),
        in_specs=[idx_spec, idx_spec,
                  pl.BlockSpec((bm, d), lambda i: (i, 0)),
                  pl.BlockSpec((bm, LANES), lambda i: (i, 0)),
                  pl.BlockSpec((1, d), lambda i: (0, 0)),
                  pl.BlockSpec(memory_space=pl.ANY)],
        out_specs=pl.BlockSpec((bm, d), lambda i: (i, 0)),
        scratch_shapes=[pltpu.VMEM((bm, d), F32), pltpu.VMEM((bm, d), F32), pltpu.SemaphoreType.DMA],
        compiler_params=_params(1),
        name="moe_combine",
    )(d1.reshape(nt, 1, bm), d2.reshape(nt, 1, bm), x, route, g.reshape(1, d), ys)


def _moe(x, g, w_router, w_gate, w_up, w_down, lead, final_gain, *, sub=256, parts=4):
    t, d = x.shape
    n_experts = w_router.shape[-1]
    route = _router(x, g, w_router)
    sup = parts * sub
    eid = route[:, :TOP_K].astype(jnp.int32).reshape(-1)
    onehot = (eid[:, None] == jnp.arange(n_experts)[None, :]).astype(jnp.int32)
    rank = jnp.sum((jnp.cumsum(onehot, axis=0) - onehot) * onehot, axis=1)
    count = jnp.sum(onehot, axis=0)
    sup_per = (count + sup - 1) // sup
    sup_end = jnp.cumsum(sup_per)
    sup_start = sup_end - sup_per
    dest = sup_start[eid] * sup + rank
    n_tiles = (TOP_K * t) // sup + n_experts
    n_super = sup_end[-1:].astype(jnp.int32)
    tile_expert = jnp.minimum(jnp.sum(jnp.arange(n_tiles)[:, None] >= sup_end[None, :], axis=1),
                              n_experts - 1).astype(jnp.int32)
    s = jnp.arange(parts * n_tiles)
    e_of = tile_expert[s // parts]
    sub_valid = (((s - parts * sup_start[e_of]) * sub < count[e_of]) & (s // parts < n_super[0])).astype(jnp.int32)
    token = jnp.arange(TOP_K * t, dtype=jnp.int32) // TOP_K
    src = jnp.zeros((n_tiles * sup,), jnp.int32).at[dest].set(token)
    xs = _gather_norm(x, g, src, sub_valid, bm=sub)
    h = _glu_up(xs, w_gate, w_up, lead, tile_expert, sub_valid, n_super, sub=sub, parts=parts, bn=512)
    ys = _glu_down(h, w_down, lead, tile_expert, sub_valid, n_super, sub=sub, parts=parts, bk=2048, bn=1024)
    dest2 = dest.reshape(t, TOP_K).astype(jnp.int32)
    gain = final_gain if final_gain is not None else jnp.ones((d,), F32)
    return _combine(x, route, ys, dest2[:, 0], dest2[:, 1], gain, final_norm=final_gain is not None)


def kernel(x_prompt, x_sample, state_ssm_re, state_ssm_im, norm_mix, norm_ffn, norm_final,
           w_in, gmlp_ln_g, gmlp_ln_b, gmlp_w_s, gmlp_b_s, w_proj_a,
           ssm_a_re, ssm_a_im, ssm_log_dt, ssm_b_re, ssm_b_im, ssm_c_re, ssm_c_im, ssm_d,
           w_glu, b_glu, w_proj_b, w_out,
           ffn_w_gate, ffn_w_up, ffn_w_down,
           moe_router, moe_w_gate, moe_w_up, moe_w_down):
    nb_, seq, d = x_prompt.shape
    db, ds, _ = x_sample.shape
    tp, ts = nb_ * seq, db * ds
    t = tp + ts
    depth = w_in.shape[0]
    gw = gmlp_ln_g.shape[1]
    sw = ssm_d.shape[1]
    groups, state = ssm_a_re.shape[1:]
    assert ds == SSM_BLOCK and gw % LANES == 0 and sw % LANES == 0 and d % LANES == 0
    xb_col = (2 * gw) // LANES
    x = jnp.concatenate([x_prompt.reshape(tp, d), x_sample.reshape(ts, d)], axis=0)

    hre_p, him_p, hre_s, him_s, v_rows = [], [], [], [], []
    for l in range(depth):
        xn = _rmsnorm(x, norm_mix[l], BF16)
        proj = _matmul(xn, w_in, (l,), F32)
        gated, v_all = _gmlp(proj, gmlp_ln_g[l], gmlp_ln_b[l], gmlp_w_s[l], gmlp_b_s[l], width=gw, tp=tp, dec_seq=ds)
        v_rows.append(v_all[tp:].reshape(db, ds, gw))
        branch_a = _matmul(gated, w_proj_a, (l,), F32)
        mats = _ssm_prep(ssm_a_re[l], ssm_a_im[l], ssm_log_dt[l], ssm_b_re[l], ssm_b_im[l], ssm_c_re[l], ssm_c_im[l])
        y_p, hr, hi = _ssm_scan(proj, mats, ssm_d[l], col_block=xb_col, row_start=0, n_seq=nb_, seq_len=seq)
        hre_p.append(hr.reshape(nb_, groups, state))
        him_p.append(hi.reshape(nb_, groups, state))
        h0 = (state_ssm_re[l].reshape(db, groups * state), state_ssm_im[l].reshape(db, groups * state))
        y_s, hr, hi = _ssm_scan(proj, mats, ssm_d[l], col_block=xb_col, row_start=tp, n_seq=db, seq_len=ds, h0=h0)
        hre_s.append(hr.reshape(db, groups, state))
        him_s.append(hi.reshape(db, groups, state))
        yg = jnp.concatenate([y_p, y_s], axis=0)
        bn_glu = _tile(sw, 512)
        y_glu = _matmul(yg, w_glu, (l,), BF16, bn=bn_glu, epilogue=_ep_glu_gate,
                        extras=[(yg, "tile", 0), (b_glu[l].reshape(1, sw), "row", 0)])
        bn_m = math.gcd(_tile(d, 512), 2 * gw + sw, 2 * gw + sw + d)
        merged = _matmul(y_glu, w_proj_b, (l,), BF16, bn=bn_m, epilogue=_ep_merge,
                         extras=[(branch_a, "tile", 0), (proj, "tile", (2 * gw + sw) // bn_m),
                                 (proj, "tile", (2 * gw + sw + d) // bn_m)])
        x = _matmul(merged, w_out, (l,), F32, epilogue=_ep_residual, extras=[(x, "tile", 0)])
        j = l // 2
        last = l == depth - 1
        if l % 2 == 0:
            xn = _rmsnorm(x, norm_ffn[l], BF16)
            sub = _tile(t, 1024, 8)
            nt = t // sub
            h = _glu_up(xn, ffn_w_gate, ffn_w_up, (), jnp.full((nt,), j, jnp.int32), jnp.ones((nt,), jnp.int32),
                        jnp.full((1,), nt, jnp.int32), sub=sub, parts=1, bn=256)
            x = _matmul(h, ffn_w_down, (j,), F32, bm=1024, bn=256, epilogue=_ep_residual, extras=[(x, "tile", 0)],
                        a_buffers=1)
            if last:
                x = _rmsnorm(x, norm_final, F32)
        else:
            x = _moe(x, norm_ffn[l], moe_router[j], moe_w_gate, moe_w_up, moe_w_down, (j,),
                     norm_final if last else None)
    y = x
    return (y[:tp].reshape(nb_, seq, d), y[tp:].reshape(db, ds, d),
            jnp.stack(hre_p), jnp.stack(him_p), jnp.stack(hre_s), jnp.stack(him_s), jnp.stack(v_rows))
```

```python
import functools
import math

import jax
import jax.numpy as jnp
from jax import lax
from jax.experimental import pallas as pl
from jax.experimental.pallas import tpu as pltpu

F32 = jnp.float32
BF16 = jnp.bfloat16

RMS_EPS = 1e-5
LN_EPS = 1e-5
TOP_K = 2

LANES = 128
SSM_BLOCK = 8
GROUPS_PER_BUNDLE = 8
V7X_VMEM_LIMIT = 56 * 1024 * 1024


def _params(n_axes, vmem=V7X_VMEM_LIMIT):
    return pltpu.CompilerParams(dimension_semantics=("arbitrary",) * n_axes, vmem_limit_bytes=vmem)


def _tile(n, pref, mult=LANES):
    if n <= pref:
        return n
    t = (pref // mult) * mult
    while t >= mult:
        if n % t == 0:
            return t
        t -= mult
    raise ValueError(f"no tile for {n} <= {pref}")


def _gelu(x):
    return 0.5 * x * (1.0 + jnp.tanh(0.7978845608028654 * (x + 0.044715 * (x * x * x))))


def _sigmoid(x):
    return 1.0 / (1.0 + jnp.exp(-x))


def _dot(a, b):
    return jnp.dot(a, b, preferred_element_type=F32)


def _dot3(a, b):
    ah = a.astype(BF16)
    al = (a - ah.astype(F32)).astype(BF16)
    bh = b.astype(BF16)
    bl = (b - bh.astype(F32)).astype(BF16)
    return _dot(ah, bh) + _dot(ah, bl) + _dot(al, bh)


def _rmsnorm_body(x_ref, g_ref, o_ref):
    x = x_ref[...]
    inv = lax.rsqrt(jnp.mean(x * x, axis=-1, keepdims=True) + RMS_EPS)
    o_ref[...] = (x * inv * g_ref[...]).astype(o_ref.dtype)


def _rmsnorm(x, g, out_dtype):
    t, d = x.shape
    bm = _tile(t, 256, 8)
    return pl.pallas_call(
        _rmsnorm_body,
        out_shape=jax.ShapeDtypeStruct((t, d), out_dtype),
        grid=(t // bm,),
        in_specs=[pl.BlockSpec((bm, d), lambda i: (i, 0)), pl.BlockSpec((1, d), lambda i: (0, 0))],
        out_specs=pl.BlockSpec((bm, d), lambda i: (i, 0)),
        compiler_params=_params(1),
        name="rmsnorm",
    )(x, g.reshape(1, d))


def _mm_body(*refs, n_extra, epilogue):
    a_ref, w_ref = refs[0], refs[1]
    extras = refs[2:2 + n_extra]
    o_ref = refs[2 + n_extra]
    acc = _dot(a_ref[...].astype(BF16), w_ref[...].astype(BF16))
    if epilogue is not None:
        acc = epilogue(acc, *[r[...] for r in extras])
    o_ref[...] = acc.astype(o_ref.dtype)


def _matmul(a, w, lead, out_dtype, *, bm=1024, bn=512, epilogue=None, extras=(), a_buffers=2):
    m, k = a.shape
    n = w.shape[-1]
    assert w.shape[-2] == k
    bm = _tile(m, bm, 8)
    bn = _tile(n, bn)
    lead = tuple(lead)
    a_kwargs = {} if a_buffers == 2 else {"pipeline_mode": pl.Buffered(a_buffers)}
    in_specs = [
        pl.BlockSpec((bm, k), lambda i, j: (i, 0), **a_kwargs),
        pl.BlockSpec((None,) * len(lead) + (k, bn), lambda i, j: lead + (0, j)),
    ]
    args = [a, w]
    for arr, kind, off in extras:
        if kind == "tile":
            in_specs.append(pl.BlockSpec((bm, bn), lambda i, j, off=off: (i, j + off)))
        else:
            in_specs.append(pl.BlockSpec((1, bn), lambda i, j: (0, j)))
        args.append(arr)
    return pl.pallas_call(
        functools.partial(_mm_body, n_extra=len(extras), epilogue=epilogue),
        out_shape=jax.ShapeDtypeStruct((m, n), out_dtype),
        grid=(m // bm, n // bn),
        in_specs=in_specs,
        out_specs=pl.BlockSpec((bm, bn), lambda i, j: (i, j)),
        compiler_params=_params(2),
        name="matmul",
    )(*args)


def _ep_residual(acc, res):
    return res + acc


def _ep_glu_gate(acc, y, b):
    return y * _sigmoid(acc + b)


def _ep_merge(acc, branch_a, g_a, g_b):
    return _sigmoid(g_a) * branch_a + _sigmoid(g_b) * acc


def _gmlp_body(u_ref, v_ref, g_ref, b_ref, w_ref, bias_ref, o_ref, vout_ref, *, heads, hd):
    v = _gelu(v_ref[...])
    mu = jnp.mean(v, axis=-1, keepdims=True)
    vc = v - mu
    var = jnp.mean(vc * vc, axis=-1, keepdims=True)
    vn = vc * lax.rsqrt(var + LN_EPS) * g_ref[...] + b_ref[...]
    vout_ref[...] = vn
    c = vn.shape[0]
    causal = lax.broadcasted_iota(jnp.int32, (c, c), 1) <= lax.broadcasted_iota(jnp.int32, (c, c), 0)
    for h in range(heads):
        sl = slice(h * hd, (h + 1) * hd)
        w = jnp.where(causal, w_ref[0, h], 0.0).astype(BF16)
        mixed = _dot(w, vn[:, sl].astype(BF16)) + bias_ref[0, :, h:h + 1]
        o_ref[:, sl] = (_gelu(u_ref[:, sl]) * mixed).astype(o_ref.dtype)


def _gmlp(proj, ln_g, ln_b, w_s, b_s, *, width, tp, dec_seq):
    t = proj.shape[0]
    heads, chunk, _ = w_s.shape
    hd = width // heads
    assert tp % chunk == 0 and (t - tp) % chunk == 0 and chunk % dec_seq == 0
    rep = chunk // dec_seq
    blk = jnp.arange(chunk) // dec_seq
    same = (blk[:, None] == blk[None, :]).astype(F32)
    w_sample = jnp.tile(w_s[:, :dec_seq, :dec_seq], (1, rep, rep)) * same[None]
    w_eff = jnp.stack([w_s, w_sample])
    bias_eff = jnp.stack([b_s.T, jnp.tile(b_s[:, :dec_seq].T, (rep, 1))])
    n_prompt_chunks = tp // chunk
    variant = lambda c: jnp.where(c >= n_prompt_chunks, 1, 0)
    return pl.pallas_call(
        functools.partial(_gmlp_body, heads=heads, hd=hd),
        out_shape=(jax.ShapeDtypeStruct((t, width), BF16), jax.ShapeDtypeStruct((t, width), F32)),
        grid=(t // chunk,),
        in_specs=[
            pl.BlockSpec((chunk, width), lambda c: (c, 0)),
            pl.BlockSpec((chunk, width), lambda c: (c, 1)),
            pl.BlockSpec((1, width), lambda c: (0, 0)),
            pl.BlockSpec((1, width), lambda c: (0, 0)),
            pl.BlockSpec((1, heads, chunk, chunk), lambda c: (variant(c), 0, 0, 0)),
            pl.BlockSpec((1, chunk, heads), lambda c: (variant(c), 0, 0)),
        ],
        out_specs=(pl.BlockSpec((chunk, width), lambda c: (c, 0)), pl.BlockSpec((chunk, width), lambda c: (c, 0))),
        compiler_params=_params(1),
        name="gmlp_spatial",
    )(proj, proj, ln_g.reshape(1, width), ln_b.reshape(1, width), w_eff, bias_eff)


def _ssm_prep_body(lr_ref, li_ref, ldt_ref, lrc_ref, lic_ref, ldtc_ref, btr_ref, bti_ref, cmr_ref, cmi_ref,
                   ge_ref, kt_ref, f_ref, apr_ref, api_ref, *, ns, ch, state):
    nb = SSM_BLOCK
    w = GROUPS_PER_BUNDLE * ch

    def powers(lr, li, ldt):
        dt = jnp.exp(ldt)
        e = jnp.exp(lr * dt)
        ar = e * jnp.cos(li * dt)
        ai = e * jnp.sin(li * dt)
        ps = [(jnp.ones_like(ar), jnp.zeros_like(ar))]
        for _ in range(nb):
            pr, pi = ps[-1]
            ps.append((pr * ar - pi * ai, pr * ai + pi * ar))
        return ar, ai, ps

    lr, li = lr_ref[0], li_ref[0]
    ar, ai, prow = powers(lr, li, ldt_ref[0])
    _, _, pcol = powers(lrc_ref[0], lic_ref[0], ldtc_ref[0])
    den = lr * lr + li * li
    rr = ((ar - 1.0) * lr + ai * li) / den
    ri = (ai * lr - (ar - 1.0) * li) / den

    mask_b = (lax.broadcasted_iota(jnp.int32, (w, ns), 0) // ch) == (lax.broadcasted_iota(jnp.int32, (w, ns), 1) // state)
    mask_c = (lax.broadcasted_iota(jnp.int32, (ns, w), 0) // state) == (lax.broadcasted_iota(jnp.int32, (ns, w), 1) // ch)
    btr, bti = btr_ref[0], bti_ref[0]
    bbr = jnp.where(mask_b, rr * btr - ri * bti, 0.0)
    bbi = jnp.where(mask_b, rr * bti + ri * btr, 0.0)
    cmr = jnp.where(mask_c, cmr_ref[0], 0.0)
    cmi = jnp.where(mask_c, cmi_ref[0], 0.0)

    kt_ref[...] = jnp.zeros(kt_ref.shape, kt_ref.dtype)
    for d in range(nb):
        pr, pi = prow[d]
        bdr = bbr * pr - bbi * pi
        bdi = bbr * pi + bbi * pr
        i = nb - 1 - d
        ge_ref[0, i * w:(i + 1) * w, 0:ns] = bdr.astype(ge_ref.dtype)
        ge_ref[0, i * w:(i + 1) * w, ns:2 * ns] = bdi.astype(ge_ref.dtype)
        kd = (_dot3(bdr, cmr) - _dot3(bdi, cmi)).astype(kt_ref.dtype)
        for i0 in range(nb - d):
            kt_ref[0, i0 * w:(i0 + 1) * w, (i0 + d) * w:(i0 + d + 1) * w] = kd
    for j in range(nb):
        pr, pi = pcol[j + 1]
        f_ref[0, 0:ns, j * w:(j + 1) * w] = (cmr * pr - cmi * pi).astype(f_ref.dtype)
        f_ref[0, ns:2 * ns, j * w:(j + 1) * w] = (-(cmr * pi + cmi * pr)).astype(f_ref.dtype)
    qr, qi = prow[nb]
    for k in range(apr_ref.shape[1]):
        apr_ref[0, k:k + 1, :] = qr
        api_ref[0, k:k + 1, :] = qi
        qr, qi = qr * qr - qi * qi, 2.0 * qr * qi


def _ssm_prep(a_re, a_im, log_dt, b_re, b_im, c_re, c_im):
    g, state = a_re.shape
    ch = b_re.shape[-1]
    gb = GROUPS_PER_BUNDLE
    assert g % gb == 0 and gb * ch == LANES
    nbun = g // gb
    ns = gb * state
    w = gb * ch
    nb = SSM_BLOCK
    ldt = jnp.repeat(log_dt, state)
    rows = [v.reshape(nbun, 1, ns) for v in (a_re, a_im, ldt)]
    cols = [v.reshape(nbun, ns, 1) for v in (a_re, a_im, ldt)]

    def b_layout(b):
        return jnp.tile(b.transpose(0, 2, 1).reshape(nbun, w, state), (1, 1, gb))

    def c_layout(c):
        cm = c.transpose(0, 2, 1).reshape(nbun, gb, state, ch).transpose(0, 2, 1, 3).reshape(nbun, state, w)
        return jnp.tile(cm, (1, gb, 1))

    spec_row = pl.BlockSpec((1, 1, ns), lambda b: (b, 0, 0))
    spec_col = pl.BlockSpec((1, ns, 1), lambda b: (b, 0, 0))
    spec_b = pl.BlockSpec((1, w, ns), lambda b: (b, 0, 0))
    spec_c = pl.BlockSpec((1, ns, w), lambda b: (b, 0, 0))
    return pl.pallas_call(
        functools.partial(_ssm_prep_body, ns=ns, ch=ch, state=state),
        out_shape=(
            jax.ShapeDtypeStruct((nbun, nb * w, 2 * ns), BF16),
            jax.ShapeDtypeStruct((nbun, nb * w, nb * w), BF16),
            jax.ShapeDtypeStruct((nbun, 2 * ns, nb * w), BF16),
            jax.ShapeDtypeStruct((nbun, 8, ns), F32),
            jax.ShapeDtypeStruct((nbun, 8, ns), F32),
        ),
        grid=(nbun,),
        in_specs=[spec_row] * 3 + [spec_col] * 3 + [spec_b] * 2 + [spec_c] * 2,
        out_specs=(
            pl.BlockSpec((1, nb * w, 2 * ns), lambda b: (b, 0, 0)),
            pl.BlockSpec((1, nb * w, nb * w), lambda b: (b, 0, 0)),
            pl.BlockSpec((1, 2 * ns, nb * w), lambda b: (b, 0, 0)),
            pl.BlockSpec((1, 8, ns), lambda b: (b, 0, 0)),
            pl.BlockSpec((1, 8, ns), lambda b: (b, 0, 0)),
        ),
        compiler_params=_params(1),
        name="ssm_prep",
    )(*rows, *cols, b_layout(b_re), b_layout(b_im), c_layout(c_re), c_layout(c_im))


def _ssm_body(*refs, rows, ns, scan_steps, has_h0):
    if has_h0:
        x_ref, ge_ref, kt_ref, f_ref, apr_ref, api_ref, d_ref, h0r_ref, h0i_ref, y_ref, hr_ref, hi_ref = refs
    else:
        x_ref, ge_ref, kt_ref, f_ref, apr_ref, api_ref, d_ref, y_ref, hr_ref, hi_ref = refs
    nb = SSM_BLOCK
    us = [x_ref[pl.ds(j, rows, stride=nb), :] for j in range(nb)]
    u = jnp.concatenate(us, axis=1).astype(BF16)
    e = _dot(u, ge_ref[0])
    cr, ci = e[:, :ns], e[:, ns:]
    if has_h0:
        pr, pi = h0r_ref[...], h0i_ref[...]
        ar, ai = apr_ref[0, 0:1, :], api_ref[0, 0:1, :]
        cr = ar * pr - ai * pi + cr
        ci = ar * pi + ai * pr + ci
        hr_ref[...] = cr
        hi_ref[...] = ci
    else:
        row = lax.broadcasted_iota(jnp.int32, (rows, 1), 0)
        for k in range(scan_steps):
            sh = 1 << k
            keep = row >= sh
            sr = jnp.where(keep, pltpu.roll(cr, sh, 0), 0.0)
            si = jnp.where(keep, pltpu.roll(ci, sh, 0), 0.0)
            ar, ai = apr_ref[0, k:k + 1, :], api_ref[0, k:k + 1, :]
            cr, ci = cr + ar * sr - ai * si, ci + ar * si + ai * sr
        keep = row >= 1
        pr = jnp.where(keep, pltpu.roll(cr, 1, 0), 0.0)
        pi = jnp.where(keep, pltpu.roll(ci, 1, 0), 0.0)
        hr_ref[...] = cr[rows - 1:rows, :]
        hi_ref[...] = ci[rows - 1:rows, :]
    carry = jnp.concatenate([pr, pi], axis=1).astype(BF16)
    y = _dot(u, kt_ref[0]) + _dot(carry, f_ref[0])
    d = d_ref[...]
    w = d.shape[-1]
    for j in range(nb):
        y_ref[pl.ds(j, rows, stride=nb), :] = _gelu(y[:, j * w:(j + 1) * w] + d * us[j])


def _ssm_scan(proj, mats, d_skip, *, col_block, row_start, n_seq, seq_len, h0=None):
    ge, kt, f, apr, api = mats
    nbun, _, ns2 = ge.shape
    ns = ns2 // 2
    w = LANES
    nb = SSM_BLOCK
    width = nbun * w
    has_h0 = h0 is not None
    if has_h0:
        assert seq_len == nb
        rows = n_seq
        steps, tok = 1, n_seq * nb
        scan_steps = 0
    else:
        rows = seq_len // nb
        steps, tok = n_seq, seq_len
        scan_steps = int(math.log2(rows))
        assert 1 << scan_steps == rows and scan_steps <= apr.shape[1]
    assert row_start % tok == 0
    r0 = row_start // tok
    mat_specs = [
        pl.BlockSpec((1,) + ge.shape[1:], lambda b, s: (b, 0, 0)),
        pl.BlockSpec((1,) + kt.shape[1:], lambda b, s: (b, 0, 0)),
        pl.BlockSpec((1,) + f.shape[1:], lambda b, s: (b, 0, 0)),
        pl.BlockSpec((1,) + apr.shape[1:], lambda b, s: (b, 0, 0)),
        pl.BlockSpec((1,) + api.shape[1:], lambda b, s: (b, 0, 0)),
        pl.BlockSpec((1, w), lambda b, s: (0, b)),
    ]
    in_specs = [pl.BlockSpec((tok, w), lambda b, s: (r0 + s, col_block + b))] + mat_specs
    args = [proj, ge, kt, f, apr, api, d_skip.reshape(1, width)]
    if has_h0:
        in_specs += [pl.BlockSpec((rows, ns), lambda b, s: (0, b))] * 2
        args += [h0[0], h0[1]]
        state_shape = jax.ShapeDtypeStruct((rows, nbun * ns), F32)
        state_spec = pl.BlockSpec((rows, ns), lambda b, s: (0, b))
    else:
        state_shape = jax.ShapeDtypeStruct((n_seq, 1, nbun * ns), F32)
        state_spec = pl.BlockSpec((None, 1, ns), lambda b, s: (s, 0, b))
    y, hr, hi = pl.pallas_call(
        functools.partial(_ssm_body, rows=rows, ns=ns, scan_steps=scan_steps, has_h0=has_h0),
        out_shape=(jax.ShapeDtypeStruct((steps * tok, width), F32), state_shape, state_shape),
        grid=(nbun, steps),
        in_specs=in_specs,
        out_specs=(pl.BlockSpec((tok, w), lambda b, s: (s, b)), state_spec, state_spec),
        compiler_params=_params(2),
        name="ssm_scan",
    )(*args)
    return y, hr.reshape(n_seq, nbun * ns), hi.reshape(n_seq, nbun * ns)


def _glu_up_body(te_ref, val_ref, ns_ref, a_ref, wg_ref, wu_ref, o_ref, wg_s, wu_s, *, sub, parts):
    i = pl.program_id(0)

    def cast_weights():
        wg_s[...] = wg_ref[...].astype(BF16)
        wu_s[...] = wu_ref[...].astype(BF16)

    def compute(rows):
        a = a_ref[rows, :]
        g = _dot(a, wg_s[...])
        u = _dot(a, wu_s[...])
        o_ref[rows, :] = (g * _sigmoid(g) * u).astype(o_ref.dtype)

    full = val_ref[parts * i + parts - 1] > 0

    @pl.when(full)
    def _():
        cast_weights()
        bm = parts * sub
        chunk = min(bm, 512)
        for r0 in range(0, bm, chunk):
            compute(slice(r0, r0 + chunk))

    @pl.when(jnp.logical_not(full))
    def _():
        for part in range(parts):
            rows = slice(part * sub, (part + 1) * sub)
            ok = val_ref[parts * i + part] > 0

            if part < parts - 1:
                @pl.when(ok)
                def _():
                    if part == 0:
                        cast_weights()
                    compute(rows)

            @pl.when(jnp.logical_not(ok))
            def _():
                o_ref[rows, :] = jnp.zeros((sub, o_ref.shape[1]), o_ref.dtype)


def _glu_up(a, w_gate, w_up, lead, tile_expert, sub_valid, n_super, *, sub, parts, bn):
    m, k = a.shape
    n = w_gate.shape[-1]
    bm = parts * sub
    nt = m // bm
    bn = _tile(n, bn)
    nj = n // bn
    lead = tuple(lead)

    def row(i, ns):
        return jnp.minimum(i, ns[0] - 1)

    def col(i, j, ns):
        return jnp.where(i < ns[0], j, nj - 1)

    w_spec = pl.BlockSpec((None,) * (len(lead) + 1) + (k, bn),
                          lambda i, j, te, val, ns: lead + (te[row(i, ns)], 0, col(i, j, ns)))
    return pl.pallas_call(
        functools.partial(_glu_up_body, sub=sub, parts=parts),
        out_shape=jax.ShapeDtypeStruct((m, n), BF16),
        grid_spec=pltpu.PrefetchScalarGridSpec(
            num_scalar_prefetch=3,
            grid=(nt, nj),
            in_specs=[pl.BlockSpec((bm, k), lambda i, j, te, val, ns: (row(i, ns), 0), pipeline_mode=pl.Buffered(1)),
                      w_spec, w_spec],
            out_specs=pl.BlockSpec((bm, bn), lambda i, j, te, val, ns: (i, j)),
            scratch_shapes=[pltpu.VMEM((k, bn), BF16), pltpu.VMEM((k, bn), BF16)],
        ),
        compiler_params=_params(2),
        name="swiglu_up",
    )(tile_expert, sub_valid, n_super, a, w_gate, w_up)


def _glu_down_body(te_ref, val_ref, ns_ref, h_ref, w_ref, o_ref, w_s, *, sub, parts):
    i, k = pl.program_id(0), pl.program_id(2)

    def cast_weights():
        w_s[...] = w_ref[...].astype(BF16)

    def accumulate(rows):
        p = _dot(h_ref[rows, :], w_s[...])

        @pl.when(k == 0)
        def _():
            o_ref[rows, :] = p

        @pl.when(k > 0)
        def _():
            o_ref[rows, :] += p

    full = val_ref[parts * i + parts - 1] > 0

    @pl.when(full)
    def _():
        cast_weights()
        accumulate(slice(0, parts * sub))

    @pl.when(jnp.logical_not(full))
    def _():
        for part in range(parts):
            rows = slice(part * sub, (part + 1) * sub)
            ok = val_ref[parts * i + part] > 0

            if part < parts - 1:
                @pl.when(ok)
                def _():
                    if part == 0:
                        cast_weights()
                    accumulate(rows)

            @pl.when(jnp.logical_and(jnp.logical_not(ok), k == 0))
            def _():
                o_ref[rows, :] = jnp.zeros((sub, o_ref.shape[1]), o_ref.dtype)


def _glu_down(h, w_down, lead, tile_expert, sub_valid, n_super, *, sub, parts, bk, bn):
    m, kdim = h.shape
    n = w_down.shape[-1]
    bm = parts * sub
    nt = m // bm
    bk = _tile(kdim, bk)
    bn = _tile(n, bn)
    nk, nn = kdim // bk, n // bn
    lead = tuple(lead)

    def row(i, ns):
        return jnp.minimum(i, ns[0] - 1)

    def frozen(i, v, last, ns):
        return jnp.where(i < ns[0], v, last)

    return pl.pallas_call(
        functools.partial(_glu_down_body, sub=sub, parts=parts),
        out_shape=jax.ShapeDtypeStruct((m, n), F32),
        grid_spec=pltpu.PrefetchScalarGridSpec(
            num_scalar_prefetch=3,
            grid=(nt, nn, nk),
            in_specs=[
                pl.BlockSpec((bm, bk), lambda i, j, k, te, val, ns: (row(i, ns), frozen(i, k, nk - 1, ns))),
                pl.BlockSpec((None,) * (len(lead) + 1) + (bk, bn),
                             lambda i, j, k, te, val, ns: lead + (te[row(i, ns)], frozen(i, k, nk - 1, ns),
                                                                  frozen(i, j, nn - 1, ns))),
            ],
            out_specs=pl.BlockSpec((bm, bn), lambda i, j, k, te, val, ns: (i, j)),
            scratch_shapes=[pltpu.VMEM((bk, bn), BF16)],
        ),
        compiler_params=_params(3),
        name="swiglu_down",
    )(tile_expert, sub_valid, n_super, h, w_down)


def _router_body(x_ref, g_ref, wr_ref, route_ref, *, n_experts):
    x = x_ref[...]
    xn = x * lax.rsqrt(jnp.mean(x * x, axis=-1, keepdims=True) + RMS_EPS) * g_ref[...]
    logits = _dot3(xn, wr_ref[...])
    lane = lax.broadcasted_iota(jnp.int32, logits.shape, 1)
    valid = lane < n_experts
    lg = jnp.where(valid, logits, -jnp.inf)
    ex = jnp.where(valid, jnp.exp(lg - jnp.max(lg, axis=-1, keepdims=True)), 0.0)
    probs = ex / jnp.sum(ex, axis=-1, keepdims=True)
    big = logits.shape[-1]
    p1 = jnp.max(probs, axis=-1, keepdims=True)
    i1 = jnp.min(jnp.where(probs == p1, lane, big), axis=-1, keepdims=True)
    rest = jnp.where(lane == i1, -1.0, jnp.where(valid, probs, -1.0))
    p2 = jnp.max(rest, axis=-1, keepdims=True)
    i2 = jnp.min(jnp.where(rest == p2, lane, big), axis=-1, keepdims=True)
    s = p1 + p2
    out = jnp.where(lane == 0, i1.astype(F32), 0.0)
    out = jnp.where(lane == 1, i2.astype(F32), out)
    out = jnp.where(lane == 2, p1 / s, out)
    out = jnp.where(lane == 3, p2 / s, out)
    route_ref[...] = out


def _router(x, g, w_router):
    t, d = x.shape
    n_experts = w_router.shape[-1]
    wr = jnp.pad(w_router, ((0, 0), (0, LANES - n_experts)))
    bm = _tile(t, 256, 8)
    return pl.pallas_call(
        functools.partial(_router_body, n_experts=n_experts),
        out_shape=jax.ShapeDtypeStruct((t, LANES), F32),
        grid=(t // bm,),
        in_specs=[pl.BlockSpec((bm, d), lambda i: (i, 0)), pl.BlockSpec((1, d), lambda i: (0, 0)),
                  pl.BlockSpec((d, LANES), lambda i: (0, 0))],
        out_specs=pl.BlockSpec((bm, LANES), lambda i: (i, 0)),
        compiler_params=_params(1),
        name="moe_router",
    )(x, g.reshape(1, d), wr)


def _row_copy(src_hbm, src_row, dst_ref, dst_row, sem):
    return pltpu.make_async_copy(src_hbm.at[pl.ds(src_row, 1), :], dst_ref.at[pl.ds(dst_row, 1), :], sem)


def _gather_norm_body(val_ref, idx_ref, x_hbm, g_ref, o_ref, buf, sem, *, bm):
    ok = val_ref[pl.program_id(0)] > 0

    @pl.when(ok)
    def _():
        def start(r, c):
            _row_copy(x_hbm, idx_ref[0, 0, r], buf, r, sem).start()
            return c

        def wait(r, c):
            _row_copy(x_hbm, 0, buf, r, sem).wait()
            return c

        lax.fori_loop(0, bm, start, 0, unroll=8)
        lax.fori_loop(0, bm, wait, 0, unroll=8)
        x = buf[...]
        inv = lax.rsqrt(jnp.mean(x * x, axis=-1, keepdims=True) + RMS_EPS)
        o_ref[...] = (x * inv * g_ref[...]).astype(o_ref.dtype)

    @pl.when(jnp.logical_not(ok))
    def _():
        o_ref[...] = jnp.zeros(o_ref.shape, o_ref.dtype)


def _gather_norm(x, g, idx, sub_valid, *, bm):
    n = idx.shape[0]
    d = x.shape[1]
    nt = n // bm
    return pl.pallas_call(
        functools.partial(_gather_norm_body, bm=bm),
        out_shape=jax.ShapeDtypeStruct((n, d), BF16),
        grid_spec=pltpu.PrefetchScalarGridSpec(
            num_scalar_prefetch=1,
            grid=(nt,),
            in_specs=[pl.BlockSpec((1, 1, bm), lambda i, val: (i, 0, 0), memory_space=pltpu.SMEM),
                      pl.BlockSpec(memory_space=pl.ANY),
                      pl.BlockSpec((1, d), lambda i, val: (0, 0))],
            out_specs=pl.BlockSpec((bm, d), lambda i, val: (i, 0)),
            scratch_shapes=[pltpu.VMEM((bm, d), F32), pltpu.SemaphoreType.DMA],
        ),
        compiler_params=_params(1),
        name="moe_gather",
    )(sub_valid, idx.reshape(nt, 1, bm), x, g.reshape(1, d))


def _combine_body(d1_ref, d2_ref, x_ref, route_ref, g_ref, ys_hbm, o_ref, buf1, buf2, sem, *, bm, final_norm):
    def start(r, c):
        _row_copy(ys_hbm, d1_ref[0, 0, r], buf1, r, sem).start()
        _row_copy(ys_hbm, d2_ref[0, 0, r], buf2, r, sem).start()
        return c

    def wait(r, c):
        _row_copy(ys_hbm, 0, buf1, r, sem).wait()
        _row_copy(ys_hbm, 0, buf2, r, sem).wait()
        return c

    lax.fori_loop(0, bm, start, 0, unroll=8)
    lax.fori_loop(0, bm, wait, 0, unroll=8)
    route = route_ref[...]
    x = x_ref[...] + route[:, 2:3] * buf1[...] + route[:, 3:4] * buf2[...]
    if final_norm:
        x = x * lax.rsqrt(jnp.mean(x * x, axis=-1, keepdims=True) + RMS_EPS) * g_ref[...]
    o_ref[...] = x


def _combine(x, route, ys, d1, d2, g, *, final_norm, bm=128):
    t, d = x.shape
    bm = _tile(t, bm, 8)
    nt = t // bm
    idx_spec = pl.BlockSpec((1, 1, bm), lambda i: (i, 0, 0), memory_space=pltpu.SMEM)
    return pl.pallas_call(
        functools.partial(_combine_body, bm=bm, final_norm=final_norm),
        out_shape=jax.ShapeDtypeStruct((t, d), F32),
        grid=(nt,),
        in_specs=[idx_spec, idx_spec,
                  pl.BlockSpec((bm, d), lambda i: (i, 0)),
                  pl.BlockSpec((bm, LANES), lambda i: (i, 0)),
                  pl.BlockSpec((1, d), lambda i: (0, 0)),
                  pl.BlockSpec(memory_space=pl.ANY)],
        out_specs=pl.BlockSpec((bm, d), lambda i: (i, 0)),
        scratch_shapes=[pltpu.VMEM((bm, d), F32), pltpu.VMEM((bm, d), F32), pltpu.SemaphoreType.DMA],
        compiler_params=_params(1),
        name="moe_combine",
    )(d1.reshape(nt, 1, bm), d2.reshape(nt, 1, bm), x, route, g.reshape(1, d), ys)


def _moe(x, g, w_router, w_gate, w_up, w_down, lead, final_gain, *, sub=256, parts=4):
    t, d = x.shape
    n_experts = w_router.shape[-1]
    route = _router(x, g, w_router)
    sup = parts * sub
    eid = route[:, :TOP_K].astype(jnp.int32).reshape(-1)
    onehot = (eid[:, None] == jnp.arange(n_experts)[None, :]).astype(jnp.int32)
    rank = jnp.sum((jnp.cumsum(onehot, axis=0) - onehot) * onehot, axis=1)
    count = jnp.sum(onehot, axis=0)
    sup_per = (count + sup - 1) // sup
    sup_end = jnp.cumsum(sup_per)
    sup_start = sup_end - sup_per
    dest = sup_start[eid] * sup + rank
    n_tiles = (TOP_K * t) // sup + n_experts
    n_super = sup_end[-1:].astype(jnp.int32)
    tile_expert = jnp.minimum(jnp.sum(jnp.arange(n_tiles)[:, None] >= sup_end[None, :], axis=1),
                              n_experts - 1).astype(jnp.int32)
    s = jnp.arange(parts * n_tiles)
    e_of = tile_expert[s // parts]
    sub_valid = (((s - parts * sup_start[e_of]) * sub < count[e_of]) & (s // parts < n_super[0])).astype(jnp.int32)
    token = jnp.arange(TOP_K * t, dtype=jnp.int32) // TOP_K
    src = jnp.zeros((n_tiles * sup,), jnp.int32).at[dest].set(token)
    xs = _gather_norm(x, g, src, sub_valid, bm=sub)
    h = _glu_up(xs, w_gate, w_up, lead, tile_expert, sub_valid, n_super, sub=sub, parts=parts, bn=512)
    ys = _glu_down(h, w_down, lead, tile_expert, sub_valid, n_super, sub=sub, parts=parts, bk=2048, bn=1024)
    dest2 = dest.reshape(t, TOP_K).astype(jnp.int32)
    gain = final_gain if final_gain is not None else jnp.ones((d,), F32)
    return _combine(x, route, ys, dest2[:, 0], dest2[:, 1], gain, final_norm=final_gain is not None)


def kernel(x_prompt, x_sample, state_ssm_re, state_ssm_im, norm_mix, norm_ffn, norm_final,
           w_in, gmlp_ln_g, gmlp_ln_b, gmlp_w_s, gmlp_b_s, w_proj_a,
           ssm_a_re, ssm_a_im, ssm_log_dt, ssm_b_re, ssm_b_im, ssm_c_re, ssm_c_im, ssm_d,
           w_glu, b_glu, w_proj_b, w_out,
           ffn_w_gate, ffn_w_up, ffn_w_down,
           moe_router, moe_w_gate, moe_w_up, moe_w_down):
    nb_, seq, d = x_prompt.shape
    db, ds, _ = x_sample.shape
    tp, ts = nb_ * seq, db * ds
    t = tp + ts
    depth = w_in.shape[0]
    gw = gmlp_ln_g.shape[1]
    sw = ssm_d.shape[1]
    groups, state = ssm_a_re.shape[1:]
    assert ds == SSM_BLOCK and gw % LANES == 0 and sw % LANES == 0 and d % LANES == 0
    xb_col = (2 * gw) // LANES
    x = jnp.concatenate([x_prompt.reshape(tp, d), x_sample.reshape(ts, d)], axis=0)

    hre_p, him_p, hre_s, him_s, v_rows = [], [], [], [], []
    for l in range(depth):
        xn = _rmsnorm(x, norm_mix[l], BF16)
        proj = _matmul(xn, w_in, (l,), F32, bm=2048, a_buffers=1)
        gated, v_all = _gmlp(proj, gmlp_ln_g[l], gmlp_ln_b[l], gmlp_w_s[l], gmlp_b_s[l], width=gw, tp=tp, dec_seq=ds)
        v_rows.append(v_all[tp:].reshape(db, ds, gw))
        branch_a = _matmul(gated, w_proj_a, (l,), F32, bm=2048, bn=1024)
        mats = _ssm_prep(ssm_a_re[l], ssm_a_im[l], ssm_log_dt[l], ssm_b_re[l], ssm_b_im[l], ssm_c_re[l], ssm_c_im[l])
        y_p, hr, hi = _ssm_scan(proj, mats, ssm_d[l], col_block=xb_col, row_start=0, n_seq=nb_, seq_len=seq)
        hre_p.append(hr.reshape(nb_, groups, state))
        him_p.append(hi.reshape(nb_, groups, state))
        h0 = (state_ssm_re[l].reshape(db, groups * state), state_ssm_im[l].reshape(db, groups * state))
        y_s, hr, hi = _ssm_scan(proj, mats, ssm_d[l], col_block=xb_col, row_start=tp, n_seq=db, seq_len=ds, h0=h0)
        hre_s.append(hr.reshape(db, groups, state))
        him_s.append(hi.reshape(db, groups, state))
        yg = jnp.concatenate([y_p, y_s], axis=0)
        bn_glu = _tile(sw, 512)
        y_glu = _matmul(yg, w_glu, (l,), BF16, bn=bn_glu, epilogue=_ep_glu_gate,
                        extras=[(yg, "tile", 0), (b_glu[l].reshape(1, sw), "row", 0)])
        bn_m = math.gcd(_tile(d, 512), 2 * gw + sw, 2 * gw + sw + d)
        merged = _matmul(y_glu, w_proj_b, (l,), BF16, bn=bn_m, epilogue=_ep_merge,
                         extras=[(branch_a, "tile", 0), (proj, "tile", (2 * gw + sw) // bn_m),
                                 (proj, "tile", (2 * gw + sw + d) // bn_m)])
        x = _matmul(merged, w_out, (l,), F32, bm=2048, a_buffers=1, epilogue=_ep_residual, extras=[(x, "tile", 0)])
        j = l // 2
        last = l == depth - 1
        if l % 2 == 0:
            xn = _rmsnorm(x, norm_ffn[l], BF16)
            sub = _tile(t, 1024, 8)
            nt = t // sub
            h = _glu_up(xn, ffn_w_gate, ffn_w_up, (), jnp.full((nt,), j, jnp.int32), jnp.ones((nt,), jnp.int32),
                        jnp.full((1,), nt, jnp.int32), sub=sub, parts=1, bn=256)
            x = _matmul(h, ffn_w_down, (j,), F32, bm=1024, bn=256, epilogue=_ep_residual, extras=[(x, "tile", 0)],
                        a_buffers=1)
            if last:
                x = _rmsnorm(x, norm_final, F32)
        else:
            x = _moe(x, norm_ffn[l], moe_router[j], moe_w_gate, moe_w_up, moe_w_down, (j,),
                     norm_final if last else None)
    y = x
    return (y[:tp].reshape(nb_, seq, d), y[tp:].reshape(db, ds, d),
            jnp.stack(hre_p), jnp.stack(him_p), jnp.stack(hre_s), jnp.stack(him_s), jnp.stack(v_rows))
```
